```python
import jax, jax.numpy as jnp
from jax import lax
import numpy as np

D_MODEL = 1024
BATCH = 16
SEQ = 4096
DEPTH = 1
DEC_BATCH = 8
DEC_SEQ = 32
PAST_LEN = 2048

CHUNK = 64
SB_HEADS = 8
SB_HEAD_DIM = 64
SB_DIM = SB_HEADS * SB_HEAD_DIM
CONV_DIM = D_MODEL - SB_DIM
CONV_WIDTH = 31
FFN_CONV_WIDTH = 3
D_FF = 2816
Q_BLOCK = 128
IN_DIM = 3 * SB_DIM + 2 * CONV_DIM
EPS = 1e-6

kernel_name = "hybrid_stickbreaking_conformer_conv_step"


def rmsnorm(x, g):
    xf = x.astype(jnp.float32)
    xf = xf * lax.rsqrt(jnp.mean(xf * xf, axis=-1, keepdims=True) + EPS)
    return (xf * g.astype(jnp.float32)).astype(x.dtype)


def layernorm(x, g, b):
    xf = x.astype(jnp.float32)
    mu = jnp.mean(xf, axis=-1, keepdims=True)
    xc = xf - mu
    xf = xc * lax.rsqrt(jnp.mean(xc * xc, axis=-1, keepdims=True) + EPS)
    return (xf * g.astype(jnp.float32) + b.astype(jnp.float32)).astype(x.dtype)


def causal_dwconv(x, state, w, b):
    width = w.shape[0]
    xp = jnp.concatenate([state.astype(x.dtype), x], axis=1)
    out = lax.conv_general_dilated(
        xp, w[:, None, :].astype(x.dtype), window_strides=(1,), padding='VALID',
        dimension_numbers=('NWC', 'WIO', 'NWC'), feature_group_count=x.shape[-1])
    return out + b.astype(x.dtype), xp[:, xp.shape[1] - (width - 1):]


def sb_attend(q, k, v, q_pos, k_pos):
    z = jnp.einsum('bqhd,bkhd->bhqk', q, k).astype(jnp.float32) * (SB_HEAD_DIM ** -0.5)
    mask = k_pos[None, :] < q_pos[:, None]
    sp = jnp.where(mask, jax.nn.softplus(z), 0.0)
    surv_incl = lax.cumsum(sp, axis=3, reverse=True)
    log_a = jax.nn.log_sigmoid(z) - (surv_incl - sp)
    a = jnp.where(mask, jnp.exp(log_a), 0.0)
    return jnp.einsum('bhqk,bkhd->bqhd', a.astype(v.dtype), v)


def head_rmsnorm(x, g):
    xf = x.astype(jnp.float32)
    xf = xf * lax.rsqrt(jnp.mean(xf * xf, axis=-1, keepdims=True) + EPS)
    return (xf * g.astype(jnp.float32)).astype(x.dtype)


def layer(x, past_k, past_v, conv_state, ffn_state, g_mix, w_in, g_q, g_k, w_dw, b_dw,
          g_conv_ln, b_conv_ln, w_out, g_ffn, w_up, w_ffn_dw, b_ffn_dw, w_down):
    B, T, _ = x.shape
    P = past_k.shape[1]
    xn = rmsnorm(x, g_mix)
    proj = xn @ w_in.astype(x.dtype)
    q, k, v, u, gate = jnp.split(
        proj, [SB_DIM, 2 * SB_DIM, 3 * SB_DIM, 3 * SB_DIM + CONV_DIM], axis=-1)
    q = head_rmsnorm(q.reshape(B, T, SB_HEADS, SB_HEAD_DIM), g_q)
    k = head_rmsnorm(k.reshape(B, T, SB_HEADS, SB_HEAD_DIM), g_k)
    v = v.reshape(B, T, SB_HEADS, SB_HEAD_DIM)
    k_all = jnp.concatenate([past_k.astype(k.dtype), k], axis=1)
    v_all = jnp.concatenate([past_v.astype(v.dtype), v], axis=1)
    k_pos = jnp.arange(P + T)
    q_pos = P + jnp.arange(T)
    if T > Q_BLOCK and T % Q_BLOCK == 0:
        nb = T // Q_BLOCK
        qb = q.reshape(B, nb, Q_BLOCK, SB_HEADS, SB_HEAD_DIM).swapaxes(0, 1)
        pb = q_pos.reshape(nb, Q_BLOCK)
        ob = lax.map(lambda a: sb_attend(a[0], k_all, v_all, a[1], k_pos), (qb, pb))
        o = ob.swapaxes(0, 1).reshape(B, T, SB_DIM)
    else:
        o = sb_attend(q, k_all, v_all, q_pos, k_pos).reshape(B, T, SB_DIM)
    c = u * jax.nn.sigmoid(gate)
    c, new_conv_state = causal_dwconv(c, conv_state, w_dw, b_dw)
    c = jax.nn.silu(layernorm(c, g_conv_ln, b_conv_ln))
    h = x + jnp.concatenate([o, c], axis=-1) @ w_out.astype(x.dtype)
    up = rmsnorm(h, g_ffn) @ w_up.astype(x.dtype)
    up_c, new_ffn_state = causal_dwconv(up, ffn_state, w_ffn_dw, b_ffn_dw)
    a, g = jnp.split(up_c, 2, axis=-1)
    y = h + (jax.nn.silu(g) * a) @ w_down.astype(x.dtype)
    return y, k, v, new_conv_state, new_ffn_state


def setup_inputs(seed: int = 0) -> dict:
    key = jax.random.key(seed)
    ks = jax.random.split(key, 20)
    f32 = jnp.float32
    n = lambda i, shape, s: jax.random.normal(ks[i], shape, f32) * s
    return {
        "x_prompt": n(0, (BATCH, SEQ, D_MODEL), 1.0),
        "x_sample": n(1, (DEC_BATCH, DEC_SEQ, D_MODEL), 1.0),
        "cache_sb_k": n(2, (DEPTH, DEC_BATCH, PAST_LEN, SB_HEADS, SB_HEAD_DIM), 1.0),
        "cache_sb_v": n(3, (DEPTH, DEC_BATCH, PAST_LEN, SB_HEADS, SB_HEAD_DIM), 1.0),
        "state_conv": n(4, (DEPTH, DEC_BATCH, CONV_WIDTH - 1, CONV_DIM), 1.0),
        "state_ffn_conv": n(5, (DEPTH, DEC_BATCH, FFN_CONV_WIDTH - 1, 2 * D_FF), 1.0),
        "g_mix": 1.0 + n(6, (DEPTH, D_MODEL), 0.05),
        "w_in": n(7, (DEPTH, D_MODEL, IN_DIM), D_MODEL ** -0.5),
        "g_q": 1.0 + n(8, (DEPTH, SB_HEAD_DIM), 0.05),
        "g_k": 1.0 + n(9, (DEPTH, SB_HEAD_DIM), 0.05),
        "w_dw": n(10, (DEPTH, CONV_WIDTH, CONV_DIM), CONV_WIDTH ** -0.5),
        "b_dw": n(11, (DEPTH, CONV_DIM), 0.02),
        "g_conv_ln": 1.0 + n(12, (DEPTH, CONV_DIM), 0.05),
        "b_conv_ln": n(13, (DEPTH, CONV_DIM), 0.02),
        "w_out": n(14, (DEPTH, D_MODEL, D_MODEL), D_MODEL ** -0.5),
        "g_ffn": 1.0 + n(15, (DEPTH, D_MODEL), 0.05),
        "w_up": n(16, (DEPTH, D_MODEL, 2 * D_FF), D_MODEL ** -0.5),
        "w_ffn_dw": n(17, (DEPTH, FFN_CONV_WIDTH, 2 * D_FF), FFN_CONV_WIDTH ** -0.5),
        "b_ffn_dw": n(18, (DEPTH, 2 * D_FF), 0.02),
        "w_down": n(19, (DEPTH, D_FF, D_MODEL), D_FF ** -0.5),
    }


def reference(x_prompt, x_sample, cache_sb_k, cache_sb_v, state_conv, state_ffn_conv,
              g_mix, w_in, g_q, g_k, w_dw, b_dw, g_conv_ln, b_conv_ln, w_out,
              g_ffn, w_up, w_ffn_dw, b_ffn_dw, w_down):
    xp, xs = x_prompt, x_sample
    Bp = xp.shape[0]
    kp_l, vp_l, ks_l, vs_l, cp_l, cs_l, fp_l, fs_l = [], [], [], [], [], [], [], []
    for l in range(DEPTH):
        w = (g_mix[l], w_in[l], g_q[l], g_k[l], w_dw[l], b_dw[l], g_conv_ln[l], b_conv_ln[l],
             w_out[l], g_ffn[l], w_up[l], w_ffn_dw[l], b_ffn_dw[l], w_down[l])
        empty = jnp.zeros((Bp, 0, SB_HEADS, SB_HEAD_DIM), xp.dtype)
        zc = jnp.zeros((Bp, CONV_WIDTH - 1, CONV_DIM), xp.dtype)
        zf = jnp.zeros((Bp, FFN_CONV_WIDTH - 1, 2 * D_FF), xp.dtype)
        xp, kp, vp, cp, fp = layer(xp, empty, empty, zc, zf, *w)
        xs, ks_, vs_, cs, fs = layer(xs, cache_sb_k[l], cache_sb_v[l], state_conv[l],
                                     state_ffn_conv[l], *w)
        kp_l.append(kp); vp_l.append(vp); ks_l.append(ks_); vs_l.append(vs_)
        cp_l.append(cp); cs_l.append(cs); fp_l.append(fp); fs_l.append(fs)
    return (xp, xs, jnp.stack(kp_l), jnp.stack(vp_l), jnp.stack(ks_l), jnp.stack(vs_l),
            jnp.stack(cp_l), jnp.stack(cs_l), jnp.stack(fp_l), jnp.stack(fs_l))
```

```python
import functools

import jax
import jax.numpy as jnp
from jax import lax
from jax.experimental import pallas as pl
from jax.experimental.pallas import tpu as pltpu

D_MODEL = 1024
SB_HEADS = 8
SB_HEAD_DIM = 64
SB_DIM = SB_HEADS * SB_HEAD_DIM
CONV_DIM = D_MODEL - SB_DIM
CONV_WIDTH = 31
FFN_CONV_WIDTH = 3
IN_DIM = 3 * SB_DIM + 2 * CONV_DIM
EPS = 1e-6

LANES = 128
HEADS_PER_STEP = LANES // SB_HEAD_DIM
CONV_HALO = 32
FFN_HALO = 8
VMEM_LIMIT_BYTES = 56 * 1024 * 1024

F32 = jnp.float32
BF16 = jnp.bfloat16


def _const_spec(shape):
    return pl.BlockSpec(shape, lambda *_: (0,) * len(shape), pipeline_mode=pl.Buffered(1))


def _inproj_kernel(x_ref, gmix_ref, win_ref, gq_ref, gk_ref, grp_ref, wdw_ref, bdw_ref, gln_ref,
                   bln_ref, cst_ref, q_ref, kb_ref, k_ref, v_ref, vt_ref, cc_ref, cso_ref,
                   cbuf, conv_buf, *, tm, tv, rows):
    @pl.when(pl.program_id(1) == 0)
    def _():
        cbuf[0:CONV_HALO, :] = cst_ref[0]

    x = x_ref[0]
    ms = jnp.mean(x * x, axis=-1, keepdims=True)
    xn = (x * lax.rsqrt(ms + EPS) * gmix_ref[...]).astype(BF16)
    proj = jnp.dot(xn, win_ref[...], preferred_element_type=F32)

    def head_rmsnorm(a, g):
        ssq = jnp.dot((a * a).astype(BF16), grp_ref[...], preferred_element_type=F32)
        return a * lax.rsqrt(ssq * (1.0 / SB_HEAD_DIM) + EPS) * g

    qn = head_rmsnorm(proj[:, 0:SB_DIM], gq_ref[...])
    kn = head_rmsnorm(proj[:, SB_DIM:2 * SB_DIM], gk_ref[...])
    v = proj[:, 2 * SB_DIM:3 * SB_DIM]
    q_ref[0] = (qn * (SB_HEAD_DIM ** -0.5)).astype(BF16)
    kb_ref[0] = kn.astype(BF16)
    k_ref[0] = kn
    v_ref[0] = v
    for i in range(tm // tv):
        vt_ref[0, i] = v[i * tv:(i + 1) * tv, :].T.astype(BF16)

    u = proj[:, 3 * SB_DIM:3 * SB_DIM + CONV_DIM]
    gate = proj[:, 3 * SB_DIM + CONV_DIM:]
    cbuf[CONV_HALO:CONV_HALO + tm, :] = u * jax.nn.sigmoid(gate)

    first = CONV_HALO - (CONV_WIDTH - 1)
    for r0 in range(0, tm, rows):
        for l0 in range(0, CONV_DIM, LANES):
            acc = jnp.broadcast_to(bdw_ref[:, l0:l0 + LANES], (rows, LANES))
            for j in range(CONV_WIDTH):
                acc = acc + wdw_ref[j:j + 1, l0:l0 + LANES] * cbuf[pl.ds(r0 + first + j, rows), l0:l0 + LANES]
            conv_buf[r0:r0 + rows, l0:l0 + LANES] = acc

    tail = cbuf[tm:tm + CONV_HALO, :]
    cso_ref[0] = tail
    cbuf[0:CONV_HALO, :] = tail

    c = conv_buf[...]
    mu = jnp.mean(c, axis=-1, keepdims=True)
    xc = c - mu
    var = jnp.mean(xc * xc, axis=-1, keepdims=True)
    y = xc * lax.rsqrt(var + EPS) * gln_ref[...] + bln_ref[...]
    cc_ref[0] = (y * jax.nn.sigmoid(y)).astype(BF16)


def _inproj(x, conv_state, p, *, tm, tv):
    b, t, _ = x.shape
    nt = t // tm
    rows = min(tm, 64)
    row_spec = lambda w: pl.BlockSpec((1, tm, w), lambda i, j: (i, j, 0))
    out_shape = (
        jax.ShapeDtypeStruct((b, t, SB_DIM), BF16),
        jax.ShapeDtypeStruct((b, t, SB_DIM), BF16),
        jax.ShapeDtypeStruct((b, t, SB_DIM), F32),
        jax.ShapeDtypeStruct((b, t, SB_DIM), F32),
        jax.ShapeDtypeStruct((b, t // tv, SB_DIM, tv), BF16),
        jax.ShapeDtypeStruct((b, t, CONV_DIM), BF16),
        jax.ShapeDtypeStruct((b, CONV_HALO, CONV_DIM), F32),
    )
    return pl.pallas_call(
        functools.partial(_inproj_kernel, tm=tm, tv=tv, rows=rows),
        grid=(b, nt),
        in_specs=[
            row_spec(D_MODEL),
            _const_spec((1, D_MODEL)),
            _const_spec((D_MODEL, IN_DIM)),
            _const_spec((1, SB_DIM)),
            _const_spec((1, SB_DIM)),
            _const_spec((SB_DIM, SB_DIM)),
            _const_spec((CONV_WIDTH, CONV_DIM)),
            _const_spec((1, CONV_DIM)),
            _const_spec((1, CONV_DIM)),
            _const_spec((1, CONV_DIM)),
            pl.BlockSpec((1, CONV_HALO, CONV_DIM), lambda i, j: (i, 0, 0)),
        ],
        out_specs=(
            row_spec(SB_DIM), row_spec(SB_DIM), row_spec(SB_DIM), row_spec(SB_DIM),
            pl.BlockSpec((1, tm // tv, SB_DIM, tv), lambda i, j: (i, j, 0, 0)),
            row_spec(CONV_DIM),
            pl.BlockSpec((1, CONV_HALO, CONV_DIM), lambda i, j: (i, 0, 0)),
        ),
        out_shape=out_shape,
        scratch_shapes=[
            pltpu.VMEM((CONV_HALO + tm, CONV_DIM), F32),
            pltpu.VMEM((tm, CONV_DIM), F32),
        ],
        compiler_params=pltpu.CompilerParams(
            dimension_semantics=("arbitrary", "arbitrary"), vmem_limit_bytes=VMEM_LIMIT_BYTES),
        name="inproj_conv",
    )(x, p["g_mix"], p["w_in"], p["g_q"], p["g_k"], p["grp"], p["w_dw"], p["b_dw"],
      p["g_conv_ln"], p["b_conv_ln"], conv_state)


def _attn_kernel(q_ref, k_ref, vt_ref, o_ref, acc_ref, carry_ref, *, tq, tk, past):
    q0 = past + pl.program_id(2) * tq
    n_full = q0 // tk

    q2 = q_ref[0]
    lane = lax.broadcasted_iota(jnp.int32, q2.shape, 1)
    q_heads = [jnp.where((lane >= h * SB_HEAD_DIM) & (lane < (h + 1) * SB_HEAD_DIM), q2, jnp.zeros_like(q2))
               for h in range(HEADS_PER_STEP)]
    suffix = (lax.broadcasted_iota(jnp.int32, (tk, tk), 1)
              >= lax.broadcasted_iota(jnp.int32, (tk, tk), 0)).astype(BF16)

    acc_ref[...] = jnp.zeros_like(acc_ref)
    carry_ref[...] = jnp.zeros_like(carry_ref)

    def key_tile(j, masked):
        k2 = k_ref[0, pl.ds(pl.multiple_of(j * tk, tk), tk), :]
        vt2 = vt_ref[0, j]
        if masked:
            kpos = j * tk + lax.broadcasted_iota(jnp.int32, (tk, tq), 0)
            qpos = q0 + lax.broadcasted_iota(jnp.int32, (tk, tq), 1)
            visible = kpos < qpos
        for h in range(HEADS_PER_STEP):
            z = lax.dot_general(k2, q_heads[h], (((1,), (1,)), ((), ())), preferred_element_type=F32)
            sp = jnp.maximum(z, 0.0) + jnp.log(1.0 + jnp.exp(-jnp.abs(z)))
            if masked:
                sp = jnp.where(visible, sp, 0.0)
            within = jnp.dot(suffix, sp.astype(BF16), preferred_element_type=F32)
            later = carry_ref[h:h + 1, :]
            a = jnp.exp(z - (within + later))
            if masked:
                a = jnp.where(visible, a, 0.0)
            hs = slice(h * SB_HEAD_DIM, (h + 1) * SB_HEAD_DIM)
            acc_ref[hs, :] += jnp.dot(vt2[hs, :], a.astype(BF16), preferred_element_type=F32)
            carry_ref[h:h + 1, :] = later + within[0:1, :]

    key_tile(n_full, True)

    def body(i, carry):
        key_tile(n_full - 1 - i, False)
        return carry

    lax.fori_loop(0, n_full, body, 0)
    o_ref[0] = acc_ref[...].T.astype(BF16)


def _attention(q, k_all, vt_all, *, tq, tk, past):
    b, t_q, _ = q.shape
    t_k = k_all.shape[1]
    nq = t_q // tq
    assert t_q % tq == 0 and past % tk == 0 and (tq == tk or nq == 1) and tq <= tk
    assert t_k % tk == 0 and t_k >= past + (nq - 1) * tq + tk
    return pl.pallas_call(
        functools.partial(_attn_kernel, tq=tq, tk=tk, past=past),
        grid=(b, SB_HEADS // HEADS_PER_STEP, nq),
        in_specs=[
            pl.BlockSpec((1, tq, LANES), lambda i, h, j: (i, j, h)),
            pl.BlockSpec((1, t_k, LANES), lambda i, h, j: (i, 0, h)),
            pl.BlockSpec((1, t_k // tk, LANES, tk), lambda i, h, j: (i, 0, h, 0)),
        ],
        out_specs=pl.BlockSpec((1, tq, LANES), lambda i, h, j: (i, j, h)),
        out_shape=jax.ShapeDtypeStruct((b, t_q, SB_DIM), BF16),
        scratch_shapes=[
            pltpu.VMEM((LANES, tq), F32),
            pltpu.VMEM((8, tq), F32),
        ],
        compiler_params=pltpu.CompilerParams(
            dimension_semantics=("arbitrary", "arbitrary", "arbitrary"), vmem_limit_bytes=VMEM_LIMIT_BYTES),
        name="sb_attention",
    )(q, k_all, vt_all)


def _ffn_kernel(x_ref, o_ref, cc_ref, wout_ref, gffn_ref, wup_ref, wfdw_ref, bfdw_ref, wdown_ref,
                fst_ref, y_ref, fso_ref, up_buf, act_buf, *, tm, d_ff, fc):
    @pl.when(pl.program_id(1) == 0)
    def _():
        up_buf[0:FFN_HALO, :] = fst_ref[0]

    h = (x_ref[0]
         + jnp.dot(o_ref[0], wout_ref[0:SB_DIM, :], preferred_element_type=F32)
         + jnp.dot(cc_ref[0], wout_ref[SB_DIM:, :], preferred_element_type=F32))
    ms = jnp.mean(h * h, axis=-1, keepdims=True)
    hn = (h * lax.rsqrt(ms + EPS) * gffn_ref[...]).astype(BF16)
    up_buf[FFN_HALO:FFN_HALO + tm, :] = jnp.dot(hn, wup_ref[...], preferred_element_type=F32)

    first = FFN_HALO - (FFN_CONV_WIDTH - 1)

    def conv(c0):
        out = jnp.broadcast_to(bfdw_ref[:, c0:c0 + fc], (tm, fc))
        for j in range(FFN_CONV_WIDTH):
            out = out + wfdw_ref[j:j + 1, c0:c0 + fc] * up_buf[pl.ds(first + j, tm), c0:c0 + fc]
        return out

    for c0 in range(0, d_ff, fc):
        a = conv(c0)
        g = conv(d_ff + c0)
        act_buf[:, c0:c0 + fc] = (g * jax.nn.sigmoid(g) * a).astype(BF16)

    tail = up_buf[tm:tm + FFN_HALO, :]
    fso_ref[0] = tail
    up_buf[0:FFN_HALO, :] = tail

    y_ref[0] = h + jnp.dot(act_buf[...], wdown_ref[...], preferred_element_type=F32)


def _ffn(x, o, cc, ffn_state, p, *, tm):
    b, t, _ = x.shape
    d_ff = p["w_down"].shape[0]
    fc = 256
    assert d_ff % fc == 0
    row_spec = lambda w: pl.BlockSpec((1, tm, w), lambda i, j: (i, j, 0))
    return pl.pallas_call(
        functools.partial(_ffn_kernel, tm=tm, d_ff=d_ff, fc=fc),
        grid=(b, t // tm),
        in_specs=[
            row_spec(D_MODEL), row_spec(SB_DIM), row_spec(CONV_DIM),
            _const_spec((D_MODEL, D_MODEL)),
            _const_spec((1, D_MODEL)),
            _const_spec((D_MODEL, 2 * d_ff)),
            _const_spec((FFN_CONV_WIDTH, 2 * d_ff)),
            _const_spec((1, 2 * d_ff)),
            _const_spec((d_ff, D_MODEL)),
            pl.BlockSpec((1, FFN_HALO, 2 * d_ff), lambda i, j: (i, 0, 0)),
        ],
        out_specs=(
            row_spec(D_MODEL),
            pl.BlockSpec((1, FFN_HALO, 2 * d_ff), lambda i, j: (i, 0, 0)),
        ),
        out_shape=(
            jax.ShapeDtypeStruct((b, t, D_MODEL), F32),
            jax.ShapeDtypeStruct((b, FFN_HALO, 2 * d_ff), F32),
        ),
        scratch_shapes=[
            pltpu.VMEM((FFN_HALO + tm, 2 * d_ff), F32),
            pltpu.VMEM((tm, d_ff), BF16),
        ],
        compiler_params=pltpu.CompilerParams(
            dimension_semantics=("arbitrary", "arbitrary"), vmem_limit_bytes=VMEM_LIMIT_BYTES),
        name="outproj_convffn",
    )(x, o, cc, p["w_out"], p["g_ffn"], p["w_up"], p["w_ffn_dw"], p["b_ffn_dw"], p["w_down"], ffn_state)


def _pad_rows_front(a, n):
    return jnp.pad(a, ((0, 0), (n - a.shape[1], 0), (0, 0)))


def _layer(x, past_k, past_v, conv_state, ffn_state, p, *, tm, tq, tk):
    b, t, _ = x.shape
    tv = min(tk, tm)
    q, kb, k, v, vt, cc, cso = _inproj(x, _pad_rows_front(conv_state, CONV_HALO), p, tm=tm, tv=tv)
    if past_k is None:
        past = 0
        o = _attention(q, kb, vt, tq=tq, tk=tk, past=0)
    else:
        past = past_k.shape[1]
        t_pad = -(-t // tk) * tk
        k_all = jnp.pad(jnp.concatenate([past_k.astype(BF16), kb], axis=1), ((0, 0), (0, t_pad - t), (0, 0)))
        v_all = jnp.pad(jnp.concatenate([past_v.astype(BF16), v.astype(BF16)], axis=1), ((0, 0), (0, t_pad - t), (0, 0)))
        vt_all = v_all.reshape(b, (past + t_pad) // tk, tk, SB_DIM).swapaxes(2, 3)
        q_pad = jnp.pad(q, ((0, 0), (0, tq - t), (0, 0)))
        o = _attention(q_pad, k_all, vt_all, tq=tq, tk=tk, past=past)[:, :t]
    y, fso = _ffn(x, o, cc, _pad_rows_front(ffn_state, FFN_HALO), p, tm=tm)
    return (y, k.reshape(b, t, SB_HEADS, SB_HEAD_DIM), v.reshape(b, t, SB_HEADS, SB_HEAD_DIM),
            cso[:, CONV_HALO - (CONV_WIDTH - 1):], fso[:, FFN_HALO - (FFN_CONV_WIDTH - 1):])


def kernel(x_prompt, x_sample, cache_sb_k, cache_sb_v, state_conv, state_ffn_conv, g_mix, w_in, g_q, g_k, w_dw, b_dw, g_conv_ln, b_conv_ln, w_out, g_ffn, w_up, w_ffn_dw, b_ffn_dw, w_down):
    depth = g_mix.shape[0]
    assert depth == 1
    bp = x_prompt.shape[0]
    bs, plen = cache_sb_k.shape[1:3]
    d_ff = w_down.shape[1]
    head_of = jnp.arange(SB_DIM) // SB_HEAD_DIM
    row = lambda a: a.reshape(1, -1)
    p = {
        "g_mix": row(g_mix[0]), "w_in": w_in[0].astype(BF16),
        "g_q": row(jnp.tile(g_q[0], SB_HEADS)), "g_k": row(jnp.tile(g_k[0], SB_HEADS)),
        "grp": (head_of[:, None] == head_of[None, :]).astype(BF16),
        "w_dw": w_dw[0], "b_dw": row(b_dw[0]),
        "g_conv_ln": row(g_conv_ln[0]), "b_conv_ln": row(b_conv_ln[0]),
        "w_out": w_out[0].astype(BF16), "g_ffn": row(g_ffn[0]),
        "w_up": w_up[0].astype(BF16), "w_ffn_dw": w_ffn_dw[0], "b_ffn_dw": row(b_ffn_dw[0]),
        "w_down": w_down[0].astype(BF16),
    }
    zc = jnp.zeros((bp, CONV_WIDTH - 1, CONV_DIM), F32)
    zf = jnp.zeros((bp, FFN_CONV_WIDTH - 1, 2 * d_ff), F32)
    yp, kp, vp, cp, fp = _layer(x_prompt, None, None, zc, zf, p, tm=256, tq=256, tk=256)
    ys, ks, vs, cs, fs = _layer(
        x_sample, cache_sb_k[0].reshape(bs, plen, SB_DIM), cache_sb_v[0].reshape(bs, plen, SB_DIM),
        state_conv[0], state_ffn_conv[0], p, tm=x_sample.shape[1], tq=LANES, tk=256)
    return (yp, ys, kp[None], vp[None], ks[None], vs[None], cp[None], cs[None], fp[None], fs[None])
```

```python
import functools

import jax
import jax.numpy as jnp
from jax import lax
from jax.experimental import pallas as pl
from jax.experimental.pallas import tpu as pltpu

D_MODEL = 1024
SB_HEADS = 8
SB_HEAD_DIM = 64
SB_DIM = SB_HEADS * SB_HEAD_DIM
CONV_DIM = D_MODEL - SB_DIM
CONV_WIDTH = 31
FFN_CONV_WIDTH = 3
IN_DIM = 3 * SB_DIM + 2 * CONV_DIM
EPS = 1e-6
LOG2_E = 1.4426950408889634

LANES = 128
SUBLANES = 8
HEADS_PER_VREG = LANES // SB_HEAD_DIM
CONV_HALO = 32
FFN_HALO = 8
VMEM_LIMIT_BYTES = 56 * 1024 * 1024

F32 = jnp.float32
BF16 = jnp.bfloat16


def _const_spec(shape):
    return pl.BlockSpec(shape, lambda *_: (0,) * len(shape), pipeline_mode=pl.Buffered(1))


def _inproj_kernel(x_ref, gmix_ref, win_ref, gq_ref, gk_ref, grp_ref, wdw_ref, bdw_ref, gln_ref,
                   bln_ref, cst_ref, q_ref, kb_ref, k_ref, v_ref, vt_ref, cc_ref, cso_ref,
                   cbuf, conv_buf, *, tm, tv, rows):
    @pl.when(pl.program_id(1) == 0)
    def _():
        cbuf[0:CONV_HALO, :] = cst_ref[0]

    x = x_ref[0]
    ms = jnp.mean(x * x, axis=-1, keepdims=True)
    xn = (x * lax.rsqrt(ms + EPS) * gmix_ref[...]).astype(BF16)
    proj = jnp.dot(xn, win_ref[...], preferred_element_type=F32)

    def head_rmsnorm(a, g):
        ssq = jnp.dot((a * a).astype(BF16), grp_ref[...], preferred_element_type=F32)
        return a * lax.rsqrt(ssq * (1.0 / SB_HEAD_DIM) + EPS) * g

    qn = head_rmsnorm(proj[:, 0:SB_DIM], gq_ref[...])
    kn = head_rmsnorm(proj[:, SB_DIM:2 * SB_DIM], gk_ref[...])
    v = proj[:, 2 * SB_DIM:3 * SB_DIM]
    q_ref[0] = (qn * (SB_HEAD_DIM ** -0.5)).astype(BF16)
    kb_ref[0] = kn.astype(BF16)
    k_ref[0] = kn
    v_ref[0] = v
    for i in range(tm // tv):
        vt_ref[0, i] = v[i * tv:(i + 1) * tv, :].T.astype(BF16)

    u = proj[:, 3 * SB_DIM:3 * SB_DIM + CONV_DIM]
    gate = proj[:, 3 * SB_DIM + CONV_DIM:]
    cbuf[CONV_HALO:CONV_HALO + tm, :] = u * jax.nn.sigmoid(gate)

    first = CONV_HALO - (CONV_WIDTH - 1)
    n_win = rows + CONV_HALO
    for r0 in range(0, tm, rows):
        for l0 in range(0, CONV_DIM, LANES):
            win = cbuf[r0:r0 + n_win, l0:l0 + LANES]
            acc = jnp.broadcast_to(bdw_ref[:, l0:l0 + LANES], (rows, LANES))
            for phase in range(SUBLANES):
                shifted = win if phase == 0 else pltpu.roll(win, n_win - phase, axis=0)
                for base in range(0, CONV_HALO + 1, SUBLANES):
                    j = base + phase - first
                    if 0 <= j < CONV_WIDTH:
                        acc = acc + wdw_ref[j:j + 1, l0:l0 + LANES] * shifted[base:base + rows, :]
            conv_buf[r0:r0 + rows, l0:l0 + LANES] = acc

    tail = cbuf[tm:tm + CONV_HALO, :]
    cso_ref[0] = tail
    cbuf[0:CONV_HALO, :] = tail

    c = conv_buf[...]
    mu = jnp.mean(c, axis=-1, keepdims=True)
    xc = c - mu
    var = jnp.mean(xc * xc, axis=-1, keepdims=True)
    y = xc * lax.rsqrt(var + EPS) * gln_ref[...] + bln_ref[...]
    cc_ref[0] = (y * jax.nn.sigmoid(y)).astype(BF16)


def _inproj(x, conv_state, p, *, tm, tv):
    b, t, _ = x.shape
    nt = t // tm
    rows = min(tm, 64)
    row_spec = lambda w: pl.BlockSpec((1, tm, w), lambda i, j: (i, j, 0))
    out_shape = (
        jax.ShapeDtypeStruct((b, t, SB_DIM), BF16),
        jax.ShapeDtypeStruct((b, t, SB_DIM), BF16),
        jax.ShapeDtypeStruct((b, t, SB_DIM), F32),
        jax.ShapeDtypeStruct((b, t, SB_DIM), F32),
        jax.ShapeDtypeStruct((b, t // tv, SB_DIM, tv), BF16),
        jax.ShapeDtypeStruct((b, t, CONV_DIM), BF16),
        jax.ShapeDtypeStruct((b, CONV_HALO, CONV_DIM), F32),
    )
    return pl.pallas_call(
        functools.partial(_inproj_kernel, tm=tm, tv=tv, rows=rows),
        grid=(b, nt),
        in_specs=[
            row_spec(D_MODEL),
            _const_spec((1, D_MODEL)),
            _const_spec((D_MODEL, IN_DIM)),
            _const_spec((1, SB_DIM)),
            _const_spec((1, SB_DIM)),
            _const_spec((SB_DIM, SB_DIM)),
            _const_spec((CONV_WIDTH, CONV_DIM)),
            _const_spec((1, CONV_DIM)),
            _const_spec((1, CONV_DIM)),
            _const_spec((1, CONV_DIM)),
            pl.BlockSpec((1, CONV_HALO, CONV_DIM), lambda i, j: (i, 0, 0)),
        ],
        out_specs=(
            row_spec(SB_DIM), row_spec(SB_DIM), row_spec(SB_DIM), row_spec(SB_DIM),
            pl.BlockSpec((1, tm // tv, SB_DIM, tv), lambda i, j: (i, j, 0, 0)),
            row_spec(CONV_DIM),
            pl.BlockSpec((1, CONV_HALO, CONV_DIM), lambda i, j: (i, 0, 0)),
        ),
        out_shape=out_shape,
        scratch_shapes=[
            pltpu.VMEM((CONV_HALO + tm, CONV_DIM), F32),
            pltpu.VMEM((tm, CONV_DIM), F32),
        ],
        compiler_params=pltpu.CompilerParams(
            dimension_semantics=("arbitrary", "arbitrary"), vmem_limit_bytes=VMEM_LIMIT_BYTES),
        name="inproj_conv",
    )(x, p["g_mix"], p["w_in"], p["g_q"], p["g_k"], p["grp"], p["w_dw"], p["b_dw"],
      p["g_conv_ln"], p["b_conv_ln"], conv_state)


def _attn_kernel(q_ref, k_ref, vt_ref, o_ref, qh_ref, acc_ref, carry_ref, *, tq, tk, past, heads):
    q0 = past + pl.program_id(2) * tq
    n_full = q0 // tk

    for h in range(heads):
        l0 = (h // HEADS_PER_VREG) * LANES
        q2 = q_ref[0, :, l0:l0 + LANES]
        lane_head = lax.broadcasted_iota(jnp.int32, q2.shape, 1) // SB_HEAD_DIM
        qh_ref[h] = jnp.where(lane_head == h % HEADS_PER_VREG, q2, jnp.zeros_like(q2))
    suffix = (lax.broadcasted_iota(jnp.int32, (tk, tk), 1)
              >= lax.broadcasted_iota(jnp.int32, (tk, tk), 0)).astype(BF16)

    acc_ref[...] = jnp.zeros_like(acc_ref)
    carry_ref[...] = jnp.zeros_like(carry_ref)

    def key_tile(j, masked):
        rows = pl.ds(pl.multiple_of(j * tk, tk), tk)
        if masked:
            kpos = j * tk + lax.broadcasted_iota(jnp.int32, (tk, tq), 0)
            qpos = q0 + lax.broadcasted_iota(jnp.int32, (tk, tq), 1)
            visible = kpos < qpos
        zs, sps, withins = [], [], []
        for h in range(heads):
            l0 = (h // HEADS_PER_VREG) * LANES
            zs.append(lax.dot_general(k_ref[0, rows, l0:l0 + LANES], qh_ref[h],
                                      (((1,), (1,)), ((), ())), preferred_element_type=F32))
        for z in zs:
            sp = jnp.maximum(z, 0.0) + jnp.log(1.0 + jnp.exp2(jnp.abs(z) * (-LOG2_E)))
            sps.append(jnp.where(visible, sp, 0.0) if masked else sp)
        for sp in sps:
            withins.append(jnp.dot(suffix, sp.astype(BF16), preferred_element_type=F32))
        for h in range(heads):
            later = carry_ref[h:h + 1, :]
            a = jnp.exp(zs[h] - (withins[h] + later))
            if masked:
                a = jnp.where(visible, a, 0.0)
            hs = slice(h * SB_HEAD_DIM, (h + 1) * SB_HEAD_DIM)
            acc_ref[hs, :] += jnp.dot(vt_ref[0, j, hs, :], a.astype(BF16), preferred_element_type=F32)
            carry_ref[h:h + 1, :] = later + withins[h][0:1, :]

    key_tile(n_full, True)

    def body(i, carry):
        key_tile(n_full - 1 - i, False)
        return carry

    lax.fori_loop(0, n_full, body, 0)
    o_ref[0] = acc_ref[...].T.astype(BF16)


def _attention(q, k_all, vt_all, *, tq, tk, past, heads):
    b, t_q, _ = q.shape
    t_k = k_all.shape[1]
    nq = t_q // tq
    width = heads * SB_HEAD_DIM
    assert t_q % tq == 0 and past % tk == 0 and (tq == tk or nq == 1) and tq <= tk
    assert t_k % tk == 0 and t_k >= past + (nq - 1) * tq + tk
    assert heads % HEADS_PER_VREG == 0 and SB_HEADS % heads == 0
    return pl.pallas_call(
        functools.partial(_attn_kernel, tq=tq, tk=tk, past=past, heads=heads),
        grid=(b, SB_HEADS // heads, nq),
        in_specs=[
            pl.BlockSpec((1, tq, width), lambda i, h, j: (i, j, h)),
            pl.BlockSpec((1, t_k, width), lambda i, h, j: (i, 0, h)),
            pl.BlockSpec((1, t_k // tk, width, tk), lambda i, h, j: (i, 0, h, 0)),
        ],
        out_specs=pl.BlockSpec((1, tq, width), lambda i, h, j: (i, j, h)),
        out_shape=jax.ShapeDtypeStruct((b, t_q, SB_DIM), BF16),
        scratch_shapes=[
            pltpu.VMEM((heads, tq, LANES), BF16),
            pltpu.VMEM((width, tq), F32),
            pltpu.VMEM((SB_HEADS, tq), F32),
        ],
        compiler_params=pltpu.CompilerParams(
            dimension_semantics=("arbitrary", "arbitrary", "arbitrary"), vmem_limit_bytes=VMEM_LIMIT_BYTES),
        name="sb_attention",
    )(q, k_all, vt_all)


def _ffn_kernel(x_ref, o_ref, cc_ref, wout_ref, gffn_ref, wup_ref, wfdw_ref, bfdw_ref, wdown_ref,
                fst_ref, y_ref, fso_ref, up_buf, act_buf, *, tm, d_ff, fc):
    @pl.when(pl.program_id(1) == 0)
    def _():
        up_buf[0:FFN_HALO, :] = fst_ref[0]

    h = (x_ref[0]
         + jnp.dot(o_ref[0], wout_ref[0:SB_DIM, :], preferred_element_type=F32)
         + jnp.dot(cc_ref[0], wout_ref[SB_DIM:, :], preferred_element_type=F32))
    ms = jnp.mean(h * h, axis=-1, keepdims=True)
    hn = (h * lax.rsqrt(ms + EPS) * gffn_ref[...]).astype(BF16)
    up_buf[FFN_HALO:FFN_HALO + tm, :] = jnp.dot(hn, wup_ref[...], preferred_element_type=F32)

    first = FFN_HALO - (FFN_CONV_WIDTH - 1)

    def conv(c0):
        out = jnp.broadcast_to(bfdw_ref[:, c0:c0 + fc], (tm, fc))
        for j in range(FFN_CONV_WIDTH):
            out = out + wfdw_ref[j:j + 1, c0:c0 + fc] * up_buf[pl.ds(first + j, tm), c0:c0 + fc]
        return out

    for c0 in range(0, d_ff, fc):
        a = conv(c0)
        g = conv(d_ff + c0)
        act_buf[:, c0:c0 + fc] = (g * jax.nn.sigmoid(g) * a).astype(BF16)

    tail = up_buf[tm:tm + FFN_HALO, :]
    fso_ref[0] = tail
    up_buf[0:FFN_HALO, :] = tail

    y_ref[0] = h + jnp.dot(act_buf[...], wdown_ref[...], preferred_element_type=F32)


def _ffn(x, o, cc, ffn_state, p, *, tm):
    b, t, _ = x.shape
    d_ff = p["w_down"].shape[0]
    fc = 256
    assert d_ff % fc == 0
    row_spec = lambda w: pl.BlockSpec((1, tm, w), lambda i, j: (i, j, 0))
    return pl.pallas_call(
        functools.partial(_ffn_kernel, tm=tm, d_ff=d_ff, fc=fc),
        grid=(b, t // tm),
        in_specs=[
            row_spec(D_MODEL), row_spec(SB_DIM), row_spec(CONV_DIM),
            _const_spec((D_MODEL, D_MODEL)),
            _const_spec((1, D_MODEL)),
            _const_spec((D_MODEL, 2 * d_ff)),
            _const_spec((FFN_CONV_WIDTH, 2 * d_ff)),
            _const_spec((1, 2 * d_ff)),
            _const_spec((d_ff, D_MODEL)),
            pl.BlockSpec((1, FFN_HALO, 2 * d_ff), lambda i, j: (i, 0, 0)),
        ],
        out_specs=(
            row_spec(D_MODEL),
            pl.BlockSpec((1, FFN_HALO, 2 * d_ff), lambda i, j: (i, 0, 0)),
        ),
        out_shape=(
            jax.ShapeDtypeStruct((b, t, D_MODEL), F32),
            jax.ShapeDtypeStruct((b, FFN_HALO, 2 * d_ff), F32),
        ),
        scratch_shapes=[
            pltpu.VMEM((FFN_HALO + tm, 2 * d_ff), F32),
            pltpu.VMEM((tm, d_ff), BF16),
        ],
        compiler_params=pltpu.CompilerParams(
            dimension_semantics=("arbitrary", "arbitrary"), vmem_limit_bytes=VMEM_LIMIT_BYTES),
        name="outproj_convffn",
    )(x, o, cc, p["w_out"], p["g_ffn"], p["w_up"], p["w_ffn_dw"], p["b_ffn_dw"], p["w_down"], ffn_state)


def _pad_rows_front(a, n):
    return jnp.pad(a, ((0, 0), (n - a.shape[1], 0), (0, 0)))


def _layer(x, past_k, past_v, conv_state, ffn_state, p, *, tm, tq, tk, heads):
    b, t, _ = x.shape
    tv = min(tk, tm)
    q, kb, k, v, vt, cc, cso = _inproj(x, _pad_rows_front(conv_state, CONV_HALO), p, tm=tm, tv=tv)
    if past_k is None:
        past = 0
        o = _attention(q, kb, vt, tq=tq, tk=tk, past=0, heads=heads)
    else:
        past = past_k.shape[1]
        t_pad = -(-t // tk) * tk
        k_all = jnp.pad(jnp.concatenate([past_k.astype(BF16), kb], axis=1), ((0, 0), (0, t_pad - t), (0, 0)))
        v_all = jnp.pad(jnp.concatenate([past_v.astype(BF16), v.astype(BF16)], axis=1), ((0, 0), (0, t_pad - t), (0, 0)))
        vt_all = v_all.reshape(b, (past + t_pad) // tk, tk, SB_DIM).swapaxes(2, 3)
        q_pad = jnp.pad(q, ((0, 0), (0, tq - t), (0, 0)))
        o = _attention(q_pad, k_all, vt_all, tq=tq, tk=tk, past=past, heads=heads)[:, :t]
    y, fso = _ffn(x, o, cc, _pad_rows_front(ffn_state, FFN_HALO), p, tm=tm)
    return (y, k.reshape(b, t, SB_HEADS, SB_HEAD_DIM), v.reshape(b, t, SB_HEADS, SB_HEAD_DIM),
            cso[:, CONV_HALO - (CONV_WIDTH - 1):], fso[:, FFN_HALO - (FFN_CONV_WIDTH - 1):])


def kernel(x_prompt, x_sample, cache_sb_k, cache_sb_v, state_conv, state_ffn_conv, g_mix, w_in, g_q, g_k, w_dw, b_dw, g_conv_ln, b_conv_ln, w_out, g_ffn, w_up, w_ffn_dw, b_ffn_dw, w_down):
    depth = g_mix.shape[0]
    assert depth == 1
    bp = x_prompt.shape[0]
    bs, plen = cache_sb_k.shape[1:3]
    d_ff = w_down.shape[1]
    head_of = jnp.arange(SB_DIM) // SB_HEAD_DIM
    row = lambda a: a.reshape(1, -1)
    p = {
        "g_mix": row(g_mix[0]), "w_in": w_in[0].astype(BF16),
        "g_q": row(jnp.tile(g_q[0], SB_HEADS)), "g_k": row(jnp.tile(g_k[0], SB_HEADS)),
        "grp": (head_of[:, None] == head_of[None, :]).astype(BF16),
        "w_dw": w_dw[0], "b_dw": row(b_dw[0]),
        "g_conv_ln": row(g_conv_ln[0]), "b_conv_ln": row(b_conv_ln[0]),
        "w_out": w_out[0].astype(BF16), "g_ffn": row(g_ffn[0]),
        "w_up": w_up[0].astype(BF16), "w_ffn_dw": w_ffn_dw[0], "b_ffn_dw": row(b_ffn_dw[0]),
        "w_down": w_down[0].astype(BF16),
    }
    zc = jnp.zeros((bp, CONV_WIDTH - 1, CONV_DIM), F32)
    zf = jnp.zeros((bp, FFN_CONV_WIDTH - 1, 2 * d_ff), F32)
    yp, kp, vp, cp, fp = _layer(x_prompt, None, None, zc, zf, p, tm=256, tq=256, tk=256, heads=SB_HEADS)
    ys, ks, vs, cs, fs = _layer(
        x_sample, cache_sb_k[0].reshape(bs, plen, SB_DIM), cache_sb_v[0].reshape(bs, plen, SB_DIM),
        state_conv[0], state_ffn_conv[0], p, tm=x_sample.shape[1], tq=LANES, tk=256, heads=SB_HEADS)
    return (yp, ys, kp[None], vp[None], ks[None], vs[None], cp[None], cs[None], fp[None], fs[None])
```

```python
import functools

import jax
import jax.numpy as jnp
from jax import lax
from jax.experimental import pallas as pl
from jax.experimental.pallas import tpu as pltpu

D_MODEL = 1024
SB_HEADS = 8
SB_HEAD_DIM = 64
SB_DIM = SB_HEADS * SB_HEAD_DIM
CONV_DIM = D_MODEL - SB_DIM
CONV_WIDTH = 31
FFN_CONV_WIDTH = 3
IN_DIM = 3 * SB_DIM + 2 * CONV_DIM
EPS = 1e-6
LOG2_E = 1.4426950408889634
F32_EXP_UNDERFLOW = 110.0
BF16_ROUNDING_SLACK = 1.02

LANES = 128
SUBLANES = 8
HEADS_PER_VREG = LANES // SB_HEAD_DIM
CONV_HALO = 32
FFN_HALO = 8
VMEM_LIMIT_BYTES = 56 * 1024 * 1024

F32 = jnp.float32
BF16 = jnp.bfloat16


def _const_spec(shape):
    return pl.BlockSpec(shape, lambda *_: (0,) * len(shape), pipeline_mode=pl.Buffered(1))


def _inproj_kernel(x_ref, gmix_ref, win_ref, gq_ref, gk_ref, grp_ref, wdw_ref, bdw_ref, gln_ref,
                   bln_ref, cst_ref, q_ref, kb_ref, k_ref, v_ref, vt_ref, cc_ref, cso_ref,
                   cbuf, conv_buf, *, tm, sub, rows):
    @pl.when(pl.program_id(1) == 0)
    def _():
        cbuf[0:CONV_HALO, :] = cst_ref[0]

    def project(m0):
        x = x_ref[0, m0:m0 + sub, :]
        ms = jnp.mean(x * x, axis=-1, keepdims=True)
        xn = (x * lax.rsqrt(ms + EPS) * gmix_ref[...]).astype(BF16)
        return jnp.dot(xn, win_ref[...], preferred_element_type=F32)

    def head_rmsnorm(a, g):
        ssq = jnp.dot((a * a).astype(BF16), grp_ref[...], preferred_element_type=F32)
        return a * lax.rsqrt(ssq * (1.0 / SB_HEAD_DIM) + EPS) * g

    def finish(m0, proj):
        out_rows = slice(m0, m0 + sub)
        qn = head_rmsnorm(proj[:, 0:SB_DIM], gq_ref[...])
        kn = head_rmsnorm(proj[:, SB_DIM:2 * SB_DIM], gk_ref[...])
        v = proj[:, 2 * SB_DIM:3 * SB_DIM]
        q_ref[0, out_rows, :] = (qn * (SB_HEAD_DIM ** -0.5)).astype(BF16)
        kb_ref[0, out_rows, :] = kn.astype(BF16)
        k_ref[0, out_rows, :] = kn
        v_ref[0, out_rows, :] = v
        vt_ref[0, m0 // sub] = v.T.astype(BF16)

        u = proj[:, 3 * SB_DIM:3 * SB_DIM + CONV_DIM]
        gate = proj[:, 3 * SB_DIM + CONV_DIM:]
        cbuf[CONV_HALO + m0:CONV_HALO + m0 + sub, :] = u * jax.nn.sigmoid(gate)

        first = CONV_HALO - (CONV_WIDTH - 1)
        n_win = rows + CONV_HALO
        for r0 in range(m0, m0 + sub, rows):
            for l0 in range(0, CONV_DIM, LANES):
                win = cbuf[r0:r0 + n_win, l0:l0 + LANES]
                acc = jnp.broadcast_to(bdw_ref[:, l0:l0 + LANES], (rows, LANES))
                for phase in range(SUBLANES):
                    shifted = win if phase == 0 else pltpu.roll(win, n_win - phase, axis=0)
                    for base in range(0, CONV_HALO + 1, SUBLANES):
                        j = base + phase - first
                        if 0 <= j < CONV_WIDTH:
                            acc = acc + wdw_ref[j:j + 1, l0:l0 + LANES] * shifted[base:base + rows, :]
                conv_buf[r0:r0 + rows, l0:l0 + LANES] = acc

        c = conv_buf[out_rows, :]
        mu = jnp.mean(c, axis=-1, keepdims=True)
        xc = c - mu
        var = jnp.mean(xc * xc, axis=-1, keepdims=True)
        y = xc * lax.rsqrt(var + EPS) * gln_ref[...] + bln_ref[...]
        cc_ref[0, out_rows, :] = (y * jax.nn.sigmoid(y)).astype(BF16)

    starts = list(range(0, tm, sub))
    proj = project(starts[0])
    for i, m0 in enumerate(starts):
        nxt = project(starts[i + 1]) if i + 1 < len(starts) else None
        finish(m0, proj)
        proj = nxt

    tail = cbuf[tm:tm + CONV_HALO, :]
    cso_ref[0] = tail
    cbuf[0:CONV_HALO, :] = tail


def _inproj(x, conv_state, p, *, tm, sub):
    b, t, _ = x.shape
    nt = t // tm
    rows = min(sub, 64)
    tv = sub
    assert tm % sub == 0
    row_spec = lambda w: pl.BlockSpec((1, tm, w), lambda i, j: (i, j, 0))
    out_shape = (
        jax.ShapeDtypeStruct((b, t, SB_DIM), BF16),
        jax.ShapeDtypeStruct((b, t, SB_DIM), BF16),
        jax.ShapeDtypeStruct((b, t, SB_DIM), F32),
        jax.ShapeDtypeStruct((b, t, SB_DIM), F32),
        jax.ShapeDtypeStruct((b, t // tv, SB_DIM, tv), BF16),
        jax.ShapeDtypeStruct((b, t, CONV_DIM), BF16),
        jax.ShapeDtypeStruct((b, CONV_HALO, CONV_DIM), F32),
    )
    return pl.pallas_call(
        functools.partial(_inproj_kernel, tm=tm, sub=sub, rows=rows),
        grid=(b, nt),
        in_specs=[
            row_spec(D_MODEL),
            _const_spec((1, D_MODEL)),
            _const_spec((D_MODEL, IN_DIM)),
            _const_spec((1, SB_DIM)),
            _const_spec((1, SB_DIM)),
            _const_spec((SB_DIM, SB_DIM)),
            _const_spec((CONV_WIDTH, CONV_DIM)),
            _const_spec((1, CONV_DIM)),
            _const_spec((1, CONV_DIM)),
            _const_spec((1, CONV_DIM)),
            pl.BlockSpec((1, CONV_HALO, CONV_DIM), lambda i, j: (i, 0, 0)),
        ],
        out_specs=(
            row_spec(SB_DIM), row_spec(SB_DIM), row_spec(SB_DIM), row_spec(SB_DIM),
            pl.BlockSpec((1, tm // tv, SB_DIM, tv), lambda i, j: (i, j, 0, 0)),
            row_spec(CONV_DIM),
            pl.BlockSpec((1, CONV_HALO, CONV_DIM), lambda i, j: (i, 0, 0)),
        ),
        out_shape=out_shape,
        scratch_shapes=[
            pltpu.VMEM((CONV_HALO + tm, CONV_DIM), F32),
            pltpu.VMEM((tm, CONV_DIM), F32),
        ],
        compiler_params=pltpu.CompilerParams(
            dimension_semantics=("arbitrary", "arbitrary"), vmem_limit_bytes=VMEM_LIMIT_BYTES),
        name="inproj_conv",
    )(x, p["g_mix"], p["w_in"], p["g_q"], p["g_k"], p["grp"], p["w_dw"], p["b_dw"],
      p["g_conv_ln"], p["b_conv_ln"], conv_state)


def _attn_kernel(stop_ref, q_ref, k_ref, vt_ref, o_ref, qh_ref, acc_ref, carry_ref, *, tq, tk, past, heads):
    q0 = past + pl.program_id(2) * tq
    n_full = q0 // tk

    for h in range(heads):
        l0 = (h // HEADS_PER_VREG) * LANES
        q2 = q_ref[0, :, l0:l0 + LANES]
        lane_head = lax.broadcasted_iota(jnp.int32, q2.shape, 1) // SB_HEAD_DIM
        qh_ref[h] = jnp.where(lane_head == h % HEADS_PER_VREG, q2, jnp.zeros_like(q2))
    suffix = (lax.broadcasted_iota(jnp.int32, (tk, tk), 1)
              >= lax.broadcasted_iota(jnp.int32, (tk, tk), 0)).astype(BF16)

    acc_ref[...] = jnp.zeros_like(acc_ref)
    carry_ref[...] = jnp.zeros_like(carry_ref)

    def key_tile(j, masked):
        rows = pl.ds(pl.multiple_of(j * tk, tk), tk)
        if masked:
            kpos = j * tk + lax.broadcasted_iota(jnp.int32, (tk, tq), 0)
            qpos = q0 + lax.broadcasted_iota(jnp.int32, (tk, tq), 1)
            visible = kpos < qpos
        zs, sps, withins = [], [], []
        for h in range(heads):
            l0 = (h // HEADS_PER_VREG) * LANES
            zs.append(lax.dot_general(k_ref[0, rows, l0:l0 + LANES], qh_ref[h],
                                      (((1,), (1,)), ((), ())), preferred_element_type=F32))
        for z in zs:
            sp = jnp.maximum(z, 0.0) + jnp.log(1.0 + jnp.exp2(jnp.abs(z) * (-LOG2_E)))
            sps.append(jnp.where(visible, sp, 0.0) if masked else sp)
        for sp in sps:
            withins.append(jnp.dot(suffix, sp.astype(BF16), preferred_element_type=F32))
        for h in range(heads):
            later = carry_ref[h:h + 1, :]
            a = jnp.exp(zs[h] - (withins[h] + later))
            if masked:
                a = jnp.where(visible, a, 0.0)
            hs = slice(h * SB_HEAD_DIM, (h + 1) * SB_HEAD_DIM)
            acc_ref[hs, :] += jnp.dot(vt_ref[0, j, hs, :], a.astype(BF16), preferred_element_type=F32)
            carry_ref[h:h + 1, :] = later + withins[h][0:1, :]

    def more_to_do():
        return jnp.min(carry_ref[...]) < stop_ref[0]

    key_tile(n_full, True)

    def cond(state):
        i, go = state
        return jnp.logical_and(i < n_full, go)

    def body(state):
        i, _ = state
        key_tile(n_full - 1 - i, False)
        return i + 1, more_to_do()

    lax.while_loop(cond, body, (jnp.int32(0), more_to_do()))
    o_ref[0] = acc_ref[...].T.astype(BF16)


def _attention(q, k_all, vt_all, stop, *, tq, tk, past, heads):
    b, t_q, _ = q.shape
    t_k = k_all.shape[1]
    nq = t_q // tq
    width = heads * SB_HEAD_DIM
    assert t_q % tq == 0 and past % tk == 0 and (tq == tk or nq == 1) and tq <= tk
    assert t_k % tk == 0 and t_k >= past + (nq - 1) * tq + tk
    assert heads % HEADS_PER_VREG == 0 and SB_HEADS % heads == 0
    return pl.pallas_call(
        functools.partial(_attn_kernel, tq=tq, tk=tk, past=past, heads=heads),
        grid=(b, SB_HEADS // heads, nq),
        in_specs=[
            pl.BlockSpec(memory_space=pltpu.SMEM),
            pl.BlockSpec((1, tq, width), lambda i, h, j: (i, j, h)),
            pl.BlockSpec((1, t_k, width), lambda i, h, j: (i, 0, h)),
            pl.BlockSpec((1, t_k // tk, width, tk), lambda i, h, j: (i, 0, h, 0)),
        ],
        out_specs=pl.BlockSpec((1, tq, width), lambda i, h, j: (i, j, h)),
        out_shape=jax.ShapeDtypeStruct((b, t_q, SB_DIM), BF16),
        scratch_shapes=[
            pltpu.VMEM((heads, tq, LANES), BF16),
            pltpu.VMEM((width, tq), F32),
            pltpu.VMEM((heads, tq), F32),
        ],
        compiler_params=pltpu.CompilerParams(
            dimension_semantics=("arbitrary", "arbitrary", "arbitrary"), vmem_limit_bytes=VMEM_LIMIT_BYTES),
        name="sb_attention",
    )(stop, q, k_all, vt_all)


def _ffn_kernel(x_ref, o_ref, cc_ref, wout_ref, gffn_ref, wup_ref, wfdw_ref, bfdw_ref, wdown_ref,
                fst_ref, y_ref, fso_ref, up_buf, act_buf, *, tm, sub, d_ff, fc):
    @pl.when(pl.program_id(1) == 0)
    def _():
        up_buf[0:FFN_HALO, :] = fst_ref[0]

    def expand(m0):
        rows = slice(m0, m0 + sub)
        h = (x_ref[0, rows, :]
             + jnp.dot(o_ref[0, rows, :], wout_ref[0:SB_DIM, :], preferred_element_type=F32)
             + jnp.dot(cc_ref[0, rows, :], wout_ref[SB_DIM:, :], preferred_element_type=F32))
        ms = jnp.mean(h * h, axis=-1, keepdims=True)
        hn = (h * lax.rsqrt(ms + EPS) * gffn_ref[...]).astype(BF16)
        up_buf[FFN_HALO + m0:FFN_HALO + m0 + sub, :] = jnp.dot(hn, wup_ref[...], preferred_element_type=F32)
        return h

    first = FFN_HALO - (FFN_CONV_WIDTH - 1)

    def conv(m0, c0):
        out = jnp.broadcast_to(bfdw_ref[:, c0:c0 + fc], (sub, fc))
        for j in range(FFN_CONV_WIDTH):
            out = out + wfdw_ref[j:j + 1, c0:c0 + fc] * up_buf[pl.ds(m0 + first + j, sub), c0:c0 + fc]
        return out

    def contract(m0, h):
        rows = slice(m0, m0 + sub)
        for c0 in range(0, d_ff, fc):
            a = conv(m0, c0)
            g = conv(m0, d_ff + c0)
            act_buf[rows, c0:c0 + fc] = (g * jax.nn.sigmoid(g) * a).astype(BF16)
        y_ref[0, rows, :] = h + jnp.dot(act_buf[rows, :], wdown_ref[...], preferred_element_type=F32)

    starts = list(range(0, tm, sub))
    h = expand(starts[0])
    for i, m0 in enumerate(starts):
        nxt = expand(starts[i + 1]) if i + 1 < len(starts) else None
        contract(m0, h)
        h = nxt

    tail = up_buf[tm:tm + FFN_HALO, :]
    fso_ref[0] = tail
    up_buf[0:FFN_HALO, :] = tail


def _ffn(x, o, cc, ffn_state, p, *, tm, sub):
    b, t, _ = x.shape
    d_ff = p["w_down"].shape[0]
    fc = 256
    assert d_ff % fc == 0 and tm % sub == 0
    row_spec = lambda w: pl.BlockSpec((1, tm, w), lambda i, j: (i, j, 0))
    return pl.pallas_call(
        functools.partial(_ffn_kernel, tm=tm, sub=sub, d_ff=d_ff, fc=fc),
        grid=(b, t // tm),
        in_specs=[
            row_spec(D_MODEL), row_spec(SB_DIM), row_spec(CONV_DIM),
            _const_spec((D_MODEL, D_MODEL)),
            _const_spec((1, D_MODEL)),
            _const_spec((D_MODEL, 2 * d_ff)),
            _const_spec((FFN_CONV_WIDTH, 2 * d_ff)),
            _const_spec((1, 2 * d_ff)),
            _const_spec((d_ff, D_MODEL)),
            pl.BlockSpec((1, FFN_HALO, 2 * d_ff), lambda i, j: (i, 0, 0)),
        ],
        out_specs=(
            row_spec(D_MODEL),
            pl.BlockSpec((1, FFN_HALO, 2 * d_ff), lambda i, j: (i, 0, 0)),
        ),
        out_shape=(
            jax.ShapeDtypeStruct((b, t, D_MODEL), F32),
            jax.ShapeDtypeStruct((b, FFN_HALO, 2 * d_ff), F32),
        ),
        scratch_shapes=[
            pltpu.VMEM((FFN_HALO + tm, 2 * d_ff), F32),
            pltpu.VMEM((tm, d_ff), BF16),
        ],
        compiler_params=pltpu.CompilerParams(
            dimension_semantics=("arbitrary", "arbitrary"), vmem_limit_bytes=VMEM_LIMIT_BYTES),
        name="outproj_convffn",
    )(x, o, cc, p["w_out"], p["g_ffn"], p["w_up"], p["w_ffn_dw"], p["b_ffn_dw"], p["w_down"], ffn_state)


def _pad_rows_front(a, n):
    return jnp.pad(a, ((0, 0), (n - a.shape[1], 0), (0, 0)))


KEY_TILE = 256
ROW_GROUP = 256
ROW_GROUPS_PER_STEP = 2


def _tiling(t):
    if t % (ROW_GROUP * ROW_GROUPS_PER_STEP) == 0:
        return ROW_GROUP * ROW_GROUPS_PER_STEP, ROW_GROUP, KEY_TILE
    assert t % SUBLANES == 0 and CONV_WIDTH - 1 <= t <= LANES
    return t, t, LANES


def _layer(x, past_k, past_v, conv_state, ffn_state, p):
    b, t, _ = x.shape
    tm, sub, tq = _tiling(t)
    tk, heads = KEY_TILE, SB_HEADS
    q, kb, k, v, vt, cc, cso = _inproj(x, _pad_rows_front(conv_state, CONV_HALO), p, tm=tm, sub=sub)
    if past_k is None:
        past = 0
        stop = (p["z_bound"] + F32_EXP_UNDERFLOW).reshape(1)
        o = _attention(q, kb, vt, stop, tq=tq, tk=tk, past=0, heads=heads)
    else:
        past = past_k.shape[1]
        t_pad = -(-t // tk) * tk
        k_all = jnp.pad(jnp.concatenate([past_k.astype(BF16), kb], axis=1), ((0, 0), (0, t_pad - t), (0, 0)))
        v_all = jnp.pad(jnp.concatenate([past_v.astype(BF16), v.astype(BF16)], axis=1), ((0, 0), (0, t_pad - t), (0, 0)))
        vt_all = v_all.reshape(b, (past + t_pad) // tk, tk, SB_DIM).swapaxes(2, 3)
        q_pad = jnp.pad(q, ((0, 0), (0, tq - t), (0, 0)))
        stop = jnp.full((1,), jnp.inf, F32)
        o = _attention(q_pad, k_all, vt_all, stop, tq=tq, tk=tk, past=past, heads=heads)[:, :t]
    y, fso = _ffn(x, o, cc, _pad_rows_front(ffn_state, FFN_HALO), p, tm=sub, sub=sub)
    return (y, k.reshape(b, t, SB_HEADS, SB_HEAD_DIM), v.reshape(b, t, SB_HEADS, SB_HEAD_DIM),
            cso[:, CONV_HALO - (CONV_WIDTH - 1):], fso[:, FFN_HALO - (FFN_CONV_WIDTH - 1):])


def kernel(x_prompt, x_sample, cache_sb_k, cache_sb_v, state_conv, state_ffn_conv, g_mix, w_in, g_q, g_k, w_dw, b_dw, g_conv_ln, b_conv_ln, w_out, g_ffn, w_up, w_ffn_dw, b_ffn_dw, w_down):
    depth = g_mix.shape[0]
    assert depth == 1
    bp = x_prompt.shape[0]
    bs, plen = cache_sb_k.shape[1:3]
    d_ff = w_down.shape[1]
    head_of = jnp.arange(SB_DIM) // SB_HEAD_DIM
    row = lambda a: a.reshape(1, -1)
    p = {
        "g_mix": row(g_mix[0]), "w_in": w_in[0].astype(BF16),
        "g_q": row(jnp.tile(g_q[0], SB_HEADS)), "g_k": row(jnp.tile(g_k[0], SB_HEADS)),
        "grp": (head_of[:, None] == head_of[None, :]).astype(BF16),
        "w_dw": w_dw[0], "b_dw": row(b_dw[0]),
        "g_conv_ln": row(g_conv_ln[0]), "b_conv_ln": row(b_conv_ln[0]),
        "w_out": w_out[0].astype(BF16), "g_ffn": row(g_ffn[0]),
        "w_up": w_up[0].astype(BF16), "w_ffn_dw": w_ffn_dw[0], "b_ffn_dw": row(b_ffn_dw[0]),
        "w_down": w_down[0].astype(BF16),
        "z_bound": BF16_ROUNDING_SLACK * SB_HEAD_DIM ** 0.5 * jnp.max(jnp.abs(g_q[0])) * jnp.max(jnp.abs(g_k[0])),
    }
    zc = jnp.zeros((bp, CONV_WIDTH - 1, CONV_DIM), F32)
    zf = jnp.zeros((bp, FFN_CONV_WIDTH - 1, 2 * d_ff), F32)
    yp, kp, vp, cp, fp = _layer(x_prompt, None, None, zc, zf, p)
    ys, ks, vs, cs, fs = _layer(
        x_sample, cache_sb_k[0].reshape(bs, plen, SB_DIM), cache_sb_v[0].reshape(bs, plen, SB_DIM),
        state_conv[0], state_ffn_conv[0], p)
    return (yp, ys, kp[None], vp[None], ks[None], vs[None], cp[None], cs[None], fp[None], fs[None])
```

```python
import functools

import jax
import jax.numpy as jnp
from jax import lax
from jax.experimental import pallas as pl
from jax.experimental.pallas import tpu as pltpu

D_MODEL = 1024
SB_HEADS = 8
SB_HEAD_DIM = 64
SB_DIM = SB_HEADS * SB_HEAD_DIM
CONV_DIM = D_MODEL - SB_DIM
CONV_WIDTH = 31
FFN_CONV_WIDTH = 3
IN_DIM = 3 * SB_DIM + 2 * CONV_DIM
EPS = 1e-6
LOG2_E = 1.4426950408889634
F32_EXP_UNDERFLOW = 110.0
BF16_ROUNDING_SLACK = 1.02

LANES = 128
SUBLANES = 8
MXU_WIDTH = 256
HEADS_PER_VREG = LANES // SB_HEAD_DIM
CONV_HALO = 32
FFN_HALO = 8
VMEM_LIMIT_BYTES = 56 * 1024 * 1024

F32 = jnp.float32
BF16 = jnp.bfloat16


def _const_spec(shape):
    return pl.BlockSpec(shape, lambda *_: (0,) * len(shape), pipeline_mode=pl.Buffered(1))


def _inproj_kernel(x_ref, gmix_ref, win_ref, gq_ref, gk_ref, grp_ref, wdw_ref, bdw_ref, gln_ref,
                   bln_ref, cst_ref, q_ref, kb_ref, k_ref, v_ref, vt_ref, cc_ref, cso_ref,
                   cbuf, conv_buf, *, tm, sub, rows):
    @pl.when(pl.program_id(1) == 0)
    def _():
        cbuf[0:CONV_HALO, :] = cst_ref[0]

    def project(m0):
        x = x_ref[0, m0:m0 + sub, :]
        ms = jnp.mean(x * x, axis=-1, keepdims=True)
        xn = (x * lax.rsqrt(ms + EPS) * gmix_ref[...]).astype(BF16)
        return jnp.dot(xn, win_ref[...], preferred_element_type=F32)

    def head_rmsnorm(a, g):
        msq = jnp.dot((a * a).astype(BF16), grp_ref[...], preferred_element_type=F32)
        return a * lax.rsqrt(msq + EPS) * g

    def finish(m0, proj):
        out_rows = slice(m0, m0 + sub)
        qn = head_rmsnorm(proj[:, 0:SB_DIM], gq_ref[...])
        kn = head_rmsnorm(proj[:, SB_DIM:2 * SB_DIM], gk_ref[...])
        v = proj[:, 2 * SB_DIM:3 * SB_DIM]
        q_ref[0, out_rows, :] = qn.astype(BF16)
        kb_ref[0, out_rows, :] = kn.astype(BF16)
        k_ref[0, out_rows, :] = kn
        v_ref[0, out_rows, :] = v
        vt_ref[0, m0 // sub] = v.T.astype(BF16)

        u = proj[:, 3 * SB_DIM:3 * SB_DIM + CONV_DIM]
        gate = proj[:, 3 * SB_DIM + CONV_DIM:]
        cbuf[CONV_HALO + m0:CONV_HALO + m0 + sub, :] = u * jax.nn.sigmoid(gate)

        first = CONV_HALO - (CONV_WIDTH - 1)
        n_win = rows + CONV_HALO
        for r0 in range(m0, m0 + sub, rows):
            for l0 in range(0, CONV_DIM, LANES):
                win = cbuf[r0:r0 + n_win, l0:l0 + LANES]
                acc = jnp.broadcast_to(bdw_ref[:, l0:l0 + LANES], (rows, LANES))
                for phase in range(SUBLANES):
                    shifted = win if phase == 0 else pltpu.roll(win, n_win - phase, axis=0)
                    for base in range(0, CONV_HALO + 1, SUBLANES):
                        j = base + phase - first
                        if 0 <= j < CONV_WIDTH:
                            acc = acc + wdw_ref[j:j + 1, l0:l0 + LANES] * shifted[base:base + rows, :]
                conv_buf[r0:r0 + rows, l0:l0 + LANES] = acc

        c = conv_buf[out_rows, :]
        mu = jnp.mean(c, axis=-1, keepdims=True)
        xc = c - mu
        var = jnp.mean(xc * xc, axis=-1, keepdims=True)
        y = xc * lax.rsqrt(var + EPS) * gln_ref[...] + bln_ref[...]
        cc_ref[0, out_rows, :] = (y * jax.nn.sigmoid(y)).astype(BF16)

    starts = list(range(0, tm, sub))
    proj = project(starts[0])
    for i, m0 in enumerate(starts):
        nxt = project(starts[i + 1]) if i + 1 < len(starts) else None
        finish(m0, proj)
        proj = nxt

    tail = cbuf[tm:tm + CONV_HALO, :]
    cso_ref[0] = tail
    cbuf[0:CONV_HALO, :] = tail


def _inproj(x, conv_state, p, *, tm, sub):
    b, t, _ = x.shape
    nt = t // tm
    rows = min(sub, 64)
    tv = sub
    assert tm % sub == 0
    row_spec = lambda w: pl.BlockSpec((1, tm, w), lambda i, j: (i, j, 0))
    out_shape = (
        jax.ShapeDtypeStruct((b, t, SB_DIM), BF16),
        jax.ShapeDtypeStruct((b, t, SB_DIM), BF16),
        jax.ShapeDtypeStruct((b, t, SB_DIM), F32),
        jax.ShapeDtypeStruct((b, t, SB_DIM), F32),
        jax.ShapeDtypeStruct((b, t // tv, SB_DIM, tv), BF16),
        jax.ShapeDtypeStruct((b, t, CONV_DIM), BF16),
        jax.ShapeDtypeStruct((b, CONV_HALO, CONV_DIM), F32),
    )
    return pl.pallas_call(
        functools.partial(_inproj_kernel, tm=tm, sub=sub, rows=rows),
        grid=(b, nt),
        in_specs=[
            row_spec(D_MODEL),
            _const_spec((1, D_MODEL)),
            _const_spec((D_MODEL, IN_DIM)),
            _const_spec((1, SB_DIM)),
            _const_spec((1, SB_DIM)),
            _const_spec((SB_DIM, SB_DIM)),
            _const_spec((CONV_WIDTH, CONV_DIM)),
            _const_spec((1, CONV_DIM)),
            _const_spec((1, CONV_DIM)),
            _const_spec((1, CONV_DIM)),
            pl.BlockSpec((1, CONV_HALO, CONV_DIM), lambda i, j: (i, 0, 0)),
        ],
        out_specs=(
            row_spec(SB_DIM), row_spec(SB_DIM), row_spec(SB_DIM), row_spec(SB_DIM),
            pl.BlockSpec((1, tm // tv, SB_DIM, tv), lambda i, j: (i, j, 0, 0)),
            row_spec(CONV_DIM),
            pl.BlockSpec((1, CONV_HALO, CONV_DIM), lambda i, j: (i, 0, 0)),
        ),
        out_shape=out_shape,
        scratch_shapes=[
            pltpu.VMEM((CONV_HALO + tm, CONV_DIM), F32),
            pltpu.VMEM((tm, CONV_DIM), F32),
        ],
        compiler_params=pltpu.CompilerParams(
            dimension_semantics=("arbitrary", "arbitrary"), vmem_limit_bytes=VMEM_LIMIT_BYTES),
        name="inproj_conv",
    )(x, p["g_mix"], p["w_in"], p["g_q"], p["g_k"], p["grp"], p["w_dw"], p["b_dw"],
      p["g_conv_ln"], p["b_conv_ln"], conv_state)


def _attn_kernel(stop_ref, q_ref, k_ref, vt_ref, o_ref, qh_ref, acc_ref, carry_ref, *, tq, tk, past, heads):
    q0 = past + pl.program_id(2) * tq
    n_full = q0 // tk

    for h in range(heads):
        l0 = (h // HEADS_PER_VREG) * LANES
        q2 = q_ref[0, :, l0:l0 + LANES]
        lane_head = lax.broadcasted_iota(jnp.int32, q2.shape, 1) // SB_HEAD_DIM
        qh_ref[h] = jnp.where(lane_head == h % HEADS_PER_VREG, q2, jnp.zeros_like(q2))
    suffix = (lax.broadcasted_iota(jnp.int32, (tk, tk), 1)
              >= lax.broadcasted_iota(jnp.int32, (tk, tk), 0)).astype(BF16)

    acc_ref[...] = jnp.zeros_like(acc_ref)
    carry_ref[...] = jnp.zeros_like(carry_ref)

    def key_tile(j, masked):
        rows = pl.ds(pl.multiple_of(j * tk, tk), tk)
        if masked:
            kpos = j * tk + lax.broadcasted_iota(jnp.int32, (tk, tq), 0)
            qpos = q0 + lax.broadcasted_iota(jnp.int32, (tk, tq), 1)
            visible = kpos < qpos
        zs, sps, withins, laters = [], [], [], []
        for h in range(heads):
            l0 = (h // HEADS_PER_VREG) * LANES
            zs.append(lax.dot_general(k_ref[0, rows, l0:l0 + LANES], qh_ref[h],
                                      (((1,), (1,)), ((), ())), preferred_element_type=F32))
        for z in zs:
            sp = jnp.maximum(z, 0.0) + jnp.log(1.0 + jnp.exp2(jnp.abs(z) * (-LOG2_E)))
            sps.append(jnp.where(visible, sp, 0.0) if masked else sp)
        for sp in sps:
            withins.append(jnp.dot(suffix, sp.astype(BF16), preferred_element_type=F32))
        for h in range(heads):
            laters.append(carry_ref[h:h + 1, :])
        carries = [laters[h] + withins[h][0:1, :] for h in range(heads)]
        more = jnp.min(functools.reduce(jnp.minimum, carries)) < stop_ref[0]
        for h in range(heads):
            a = jnp.exp(zs[h] - (withins[h] + laters[h]))
            if masked:
                a = jnp.where(visible, a, 0.0)
            hs = slice(h * SB_HEAD_DIM, (h + 1) * SB_HEAD_DIM)
            acc_ref[hs, :] += jnp.dot(vt_ref[0, j, hs, :], a.astype(BF16), preferred_element_type=F32)
            carry_ref[h:h + 1, :] = carries[h]
        return more

    def cond(state):
        i, more = state
        return jnp.logical_and(i < n_full, more)

    def body(state):
        i, _ = state
        return i + 1, key_tile(n_full - 1 - i, False)

    lax.while_loop(cond, body, (jnp.int32(0), key_tile(n_full, True)))
    o_ref[0] = acc_ref[...].T.astype(BF16)


def _attention(q, k_all, vt_all, stop, *, tq, tk, past, heads):
    b, t_q, _ = q.shape
    t_k = k_all.shape[1]
    nq = t_q // tq
    width = heads * SB_HEAD_DIM
    assert t_q % tq == 0 and past % tk == 0 and (tq == tk or nq == 1) and tq <= tk
    assert t_k % tk == 0 and t_k >= past + (nq - 1) * tq + tk
    assert heads % HEADS_PER_VREG == 0 and SB_HEADS % heads == 0
    return pl.pallas_call(
        functools.partial(_attn_kernel, tq=tq, tk=tk, past=past, heads=heads),
        grid=(b, SB_HEADS // heads, nq),
        in_specs=[
            pl.BlockSpec(memory_space=pltpu.SMEM),
            pl.BlockSpec((1, tq, width), lambda i, h, j: (i, j, h)),
            pl.BlockSpec((1, t_k, width), lambda i, h, j: (i, 0, h)),
            pl.BlockSpec((1, t_k // tk, width, tk), lambda i, h, j: (i, 0, h, 0)),
        ],
        out_specs=pl.BlockSpec((1, tq, width), lambda i, h, j: (i, j, h)),
        out_shape=jax.ShapeDtypeStruct((b, t_q, SB_DIM), BF16),
        scratch_shapes=[
            pltpu.VMEM((heads, tq, LANES), BF16),
            pltpu.VMEM((width, tq), F32),
            pltpu.VMEM((heads, tq), F32),
        ],
        compiler_params=pltpu.CompilerParams(
            dimension_semantics=("arbitrary", "arbitrary", "arbitrary"), vmem_limit_bytes=VMEM_LIMIT_BYTES),
        name="sb_attention",
    )(stop, q, k_all, vt_all)


def _ffn_kernel(x_ref, o_ref, cc_ref, wout_ref, gffn_ref, wup_ref, wfdw_ref, bfdw_ref, wdown_ref,
                fst_ref, y_ref, fso_ref, up_buf, act_buf, *, tm, d_ff, fc):
    @pl.when(pl.program_id(1) == 0)
    def _():
        up_buf[0:FFN_HALO, :] = fst_ref[0]

    h = (x_ref[0]
         + jnp.dot(o_ref[0], wout_ref[0:SB_DIM, :], preferred_element_type=F32)
         + jnp.dot(cc_ref[0], wout_ref[SB_DIM:, :], preferred_element_type=F32))
    ms = jnp.mean(h * h, axis=-1, keepdims=True)
    hn = (h * lax.rsqrt(ms + EPS) * gffn_ref[...]).astype(BF16)
    up_buf[FFN_HALO:FFN_HALO + tm, :] = jnp.dot(hn, wup_ref[...], preferred_element_type=F32)

    first = FFN_HALO - (FFN_CONV_WIDTH - 1)

    def conv(c0):
        out = jnp.broadcast_to(bfdw_ref[:, c0:c0 + fc], (tm, fc))
        for j in range(FFN_CONV_WIDTH):
            out = out + wfdw_ref[j:j + 1, c0:c0 + fc] * up_buf[pl.ds(first + j, tm), c0:c0 + fc]
        return out

    for c0 in range(0, d_ff, fc):
        a = conv(c0)
        g = conv(d_ff + c0)
        act_buf[:, c0:c0 + fc] = (g * jax.nn.sigmoid(g) * a).astype(BF16)
    y_ref[0] = h + jnp.dot(act_buf[...], wdown_ref[...], preferred_element_type=F32)

    tail = up_buf[tm:tm + FFN_HALO, :]
    fso_ref[0] = tail
    up_buf[0:FFN_HALO, :] = tail


def _ffn(x, o, cc, ffn_state, p, *, tm):
    b, t, _ = x.shape
    d_ff = p["w_down"].shape[0]
    fc = MXU_WIDTH
    assert d_ff % fc == 0 and t % tm == 0
    row_spec = lambda w: pl.BlockSpec((1, tm, w), lambda i, j: (i, j, 0))
    return pl.pallas_call(
        functools.partial(_ffn_kernel, tm=tm, d_ff=d_ff, fc=fc),
        grid=(b, t // tm),
        in_specs=[
            row_spec(D_MODEL), row_spec(SB_DIM), row_spec(CONV_DIM),
            _const_spec((D_MODEL, D_MODEL)),
            _const_spec((1, D_MODEL)),
            _const_spec((D_MODEL, 2 * d_ff)),
            _const_spec((FFN_CONV_WIDTH, 2 * d_ff)),
            _const_spec((1, 2 * d_ff)),
            _const_spec((d_ff, D_MODEL)),
            pl.BlockSpec((1, FFN_HALO, 2 * d_ff), lambda i, j: (i, 0, 0)),
        ],
        out_specs=(
            row_spec(D_MODEL),
            pl.BlockSpec((1, FFN_HALO, 2 * d_ff), lambda i, j: (i, 0, 0)),
        ),
        out_shape=(
            jax.ShapeDtypeStruct((b, t, D_MODEL), F32),
            jax.ShapeDtypeStruct((b, FFN_HALO, 2 * d_ff), F32),
        ),
        scratch_shapes=[
            pltpu.VMEM((FFN_HALO + tm, 2 * d_ff), F32),
            pltpu.VMEM((tm, d_ff), BF16),
        ],
        compiler_params=pltpu.CompilerParams(
            dimension_semantics=("arbitrary", "arbitrary"), vmem_limit_bytes=VMEM_LIMIT_BYTES),
        name="outproj_convffn",
    )(x, o, cc, p["w_out"], p["g_ffn"], p["w_up"], p["w_ffn_dw"], p["b_ffn_dw"], p["w_down"], ffn_state)


def _pad_rows_front(a, n):
    return jnp.pad(a, ((0, 0), (n - a.shape[1], 0), (0, 0)))


KEY_TILE = 256
ROW_GROUP = 256
INPROJ_ROW_GROUPS = 2


def _tiling(t):
    if t % (ROW_GROUP * INPROJ_ROW_GROUPS) == 0:
        return ROW_GROUP * INPROJ_ROW_GROUPS, ROW_GROUP, KEY_TILE
    assert t % SUBLANES == 0 and CONV_WIDTH - 1 <= t <= LANES
    return t, t, LANES


def _layer(x, past_k, past_v, conv_state, ffn_state, p):
    b, t, _ = x.shape
    tm, sub, tq = _tiling(t)
    tk, heads = KEY_TILE, SB_HEADS
    q, kb, k, v, vt, cc, cso = _inproj(x, _pad_rows_front(conv_state, CONV_HALO), p, tm=tm, sub=sub)
    if past_k is None:
        past = 0
        stop = (BF16_ROUNDING_SLACK * p["q_norm_bound"] * p["k_norm_bound"] + F32_EXP_UNDERFLOW).reshape(1)
        o = _attention(q, kb, vt, stop, tq=tq, tk=tk, past=0, heads=heads)
    else:
        past = past_k.shape[1]
        t_pad = -(-t // tk) * tk
        k_all = jnp.pad(jnp.concatenate([past_k.astype(BF16), kb], axis=1), ((0, 0), (0, t_pad - t), (0, 0)))
        v_all = jnp.pad(jnp.concatenate([past_v.astype(BF16), v.astype(BF16)], axis=1), ((0, 0), (0, t_pad - t), (0, 0)))
        vt_all = v_all.reshape(b, (past + t_pad) // tk, tk, SB_DIM).swapaxes(2, 3)
        q_pad = jnp.pad(q, ((0, 0), (0, tq - t), (0, 0)))
        cached = past_k.reshape(b, past, SB_HEADS, SB_HEAD_DIM)
        k_norm = jnp.sqrt(jnp.max(jnp.sum(cached * cached, axis=-1)))
        stop = (BF16_ROUNDING_SLACK * p["q_norm_bound"] * jnp.maximum(k_norm, p["k_norm_bound"])
                + F32_EXP_UNDERFLOW).reshape(1)
        o = _attention(q_pad, k_all, vt_all, stop, tq=tq, tk=tk, past=past, heads=heads)[:, :t]
    y, fso = _ffn(x, o, cc, _pad_rows_front(ffn_state, FFN_HALO), p, tm=sub)
    return (y, k.reshape(b, t, SB_HEADS, SB_HEAD_DIM), v.reshape(b, t, SB_HEADS, SB_HEAD_DIM),
            cso[:, CONV_HALO - (CONV_WIDTH - 1):], fso[:, FFN_HALO - (FFN_CONV_WIDTH - 1):])


def kernel(x_prompt, x_sample, cache_sb_k, cache_sb_v, state_conv, state_ffn_conv, g_mix, w_in, g_q, g_k, w_dw, b_dw, g_conv_ln, b_conv_ln, w_out, g_ffn, w_up, w_ffn_dw, b_ffn_dw, w_down):
    depth = g_mix.shape[0]
    assert depth == 1
    bp = x_prompt.shape[0]
    bs, plen = cache_sb_k.shape[1:3]
    d_ff = w_down.shape[1]
    head_of = jnp.arange(SB_DIM) // SB_HEAD_DIM
    row = lambda a: a.reshape(1, -1)
    p = {
        "g_mix": row(g_mix[0]), "w_in": w_in[0].astype(BF16),
        "g_q": row(jnp.tile(g_q[0], SB_HEADS)) * SB_HEAD_DIM ** -0.5, "g_k": row(jnp.tile(g_k[0], SB_HEADS)),
        "grp": ((head_of[:, None] == head_of[None, :]).astype(F32) / SB_HEAD_DIM).astype(BF16),
        "w_dw": w_dw[0], "b_dw": row(b_dw[0]),
        "g_conv_ln": row(g_conv_ln[0]), "b_conv_ln": row(b_conv_ln[0]),
        "w_out": w_out[0].astype(BF16), "g_ffn": row(g_ffn[0]),
        "w_up": w_up[0].astype(BF16), "w_ffn_dw": w_ffn_dw[0], "b_ffn_dw": row(b_ffn_dw[0]),
        "w_down": w_down[0].astype(BF16),
        "q_norm_bound": jnp.max(jnp.abs(g_q[0])),
        "k_norm_bound": SB_HEAD_DIM ** 0.5 * jnp.max(jnp.abs(g_k[0])),
    }
    zc = jnp.zeros((bp, CONV_WIDTH - 1, CONV_DIM), F32)
    zf = jnp.zeros((bp, FFN_CONV_WIDTH - 1, 2 * d_ff), F32)
    yp, kp, vp, cp, fp = _layer(x_prompt, None, None, zc, zf, p)
    ys, ks, vs, cs, fs = _layer(
        x_sample, cache_sb_k[0].reshape(bs, plen, SB_DIM), cache_sb_v[0].reshape(bs, plen, SB_DIM),
        state_conv[0], state_ffn_conv[0], p)
    return (yp, ys, kp[None], vp[None], ks[None], vs[None], cp[None], cs[None], fp[None], fs[None])
```

```python
import functools

import jax
import jax.numpy as jnp
from jax import lax
from jax.experimental import pallas as pl
from jax.experimental.pallas import tpu as pltpu

D_MODEL = 1024
SB_HEADS = 8
SB_HEAD_DIM = 64
SB_DIM = SB_HEADS * SB_HEAD_DIM
CONV_DIM = D_MODEL - SB_DIM
CONV_WIDTH = 31
FFN_CONV_WIDTH = 3
IN_DIM = 3 * SB_DIM + 2 * CONV_DIM
EPS = 1e-6
LOG2_E = 1.4426950408889634
F32_EXP_UNDERFLOW = 110.0
BF16_ROUNDING_SLACK = 1.02

LANES = 128
SUBLANES = 8
MXU_WIDTH = 256
HEADS_PER_VREG = LANES // SB_HEAD_DIM
CONV_HALO = 32
FFN_HALO = 8
VMEM_LIMIT_BYTES = 56 * 1024 * 1024
CAST_BLOCK_ROWS = 256

F32 = jnp.float32
BF16 = jnp.bfloat16


def _sigmoid(x):
    return 0.5 * jnp.tanh(0.5 * x) + 0.5


def _swish(x):
    half = 0.5 * x
    return half * jnp.tanh(half) + half


def _const_spec(shape):
    return pl.BlockSpec(shape, lambda *_: (0,) * len(shape), pipeline_mode=pl.Buffered(1))


def _to_bf16_kernel(w_ref, o_ref):
    o_ref[...] = w_ref[...].astype(BF16)


def _to_bf16(w):
    rows, cols = w.shape
    block = min(rows, CAST_BLOCK_ROWS)
    assert rows % block == 0
    return pl.pallas_call(
        _to_bf16_kernel,
        grid=(rows // block,),
        in_specs=[pl.BlockSpec((block, cols), lambda i: (i, 0))],
        out_specs=pl.BlockSpec((block, cols), lambda i: (i, 0)),
        out_shape=jax.ShapeDtypeStruct((rows, cols), BF16),
        compiler_params=pltpu.CompilerParams(
            dimension_semantics=("arbitrary",), vmem_limit_bytes=VMEM_LIMIT_BYTES),
        name="weight_to_bf16",
    )(w)


def _inproj_kernel(x_ref, gmix_ref, win_ref, gq_ref, gk_ref, grp_ref, wdw_ref, bdw_ref, gln_ref,
                   bln_ref, cst_ref, q_ref, kb_ref, k_ref, v_ref, vt_ref, cc_ref, cso_ref,
                   cbuf, conv_buf, *, tm, sub, rows):
    @pl.when(pl.program_id(1) == 0)
    def _():
        cbuf[0:CONV_HALO, :] = cst_ref[0]

    def project(m0):
        x = x_ref[0, m0:m0 + sub, :]
        ms = jnp.mean(x * x, axis=-1, keepdims=True)
        xn = (x * lax.rsqrt(ms + EPS) * gmix_ref[...]).astype(BF16)
        return jnp.dot(xn, win_ref[...], preferred_element_type=F32)

    def head_rmsnorm(a, g):
        msq = jnp.dot((a * a).astype(BF16), grp_ref[...], preferred_element_type=F32)
        return a * lax.rsqrt(msq + EPS) * g

    def finish(m0, proj):
        out_rows = slice(m0, m0 + sub)
        qn = head_rmsnorm(proj[:, 0:SB_DIM], gq_ref[...])
        kn = head_rmsnorm(proj[:, SB_DIM:2 * SB_DIM], gk_ref[...])
        v = proj[:, 2 * SB_DIM:3 * SB_DIM]
        q_ref[0, out_rows, :] = qn.astype(BF16)
        kb_ref[0, out_rows, :] = kn.astype(BF16)
        k_ref[0, out_rows, :] = kn
        v_ref[0, out_rows, :] = v
        vt_ref[0, m0 // sub] = v.T.astype(BF16)

        u = proj[:, 3 * SB_DIM:3 * SB_DIM + CONV_DIM]
        gate = proj[:, 3 * SB_DIM + CONV_DIM:]
        cbuf[CONV_HALO + m0:CONV_HALO + m0 + sub, :] = u * _sigmoid(gate)

        first = CONV_HALO - (CONV_WIDTH - 1)
        n_win = rows + CONV_HALO
        for r0 in range(m0, m0 + sub, rows):
            for l0 in range(0, CONV_DIM, LANES):
                win = cbuf[r0:r0 + n_win, l0:l0 + LANES]
                acc = jnp.broadcast_to(bdw_ref[:, l0:l0 + LANES], (rows, LANES))
                for phase in range(SUBLANES):
                    shifted = win if phase == 0 else pltpu.roll(win, n_win - phase, axis=0)
                    for base in range(0, CONV_HALO + 1, SUBLANES):
                        j = base + phase - first
                        if 0 <= j < CONV_WIDTH:
                            acc = acc + wdw_ref[j:j + 1, l0:l0 + LANES] * shifted[base:base + rows, :]
                conv_buf[r0:r0 + rows, l0:l0 + LANES] = acc

        c = conv_buf[out_rows, :]
        mu = jnp.mean(c, axis=-1, keepdims=True)
        xc = c - mu
        var = jnp.mean(xc * xc, axis=-1, keepdims=True)
        y = xc * lax.rsqrt(var + EPS) * gln_ref[...] + bln_ref[...]
        cc_ref[0, out_rows, :] = _swish(y).astype(BF16)

    starts = list(range(0, tm, sub))
    proj = project(starts[0])
    for i, m0 in enumerate(starts):
        nxt = project(starts[i + 1]) if i + 1 < len(starts) else None
        finish(m0, proj)
        proj = nxt

    tail = cbuf[tm:tm + CONV_HALO, :]
    cso_ref[0] = tail
    cbuf[0:CONV_HALO, :] = tail


def _inproj(x, conv_state, p, *, tm, sub):
    b, t, _ = x.shape
    nt = t // tm
    rows = min(sub, 64)
    tv = sub
    assert tm % sub == 0
    row_spec = lambda w: pl.BlockSpec((1, tm, w), lambda i, j: (i, j, 0))
    out_shape = (
        jax.ShapeDtypeStruct((b, t, SB_DIM), BF16),
        jax.ShapeDtypeStruct((b, t, SB_DIM), BF16),
        jax.ShapeDtypeStruct((b, t, SB_DIM), F32),
        jax.ShapeDtypeStruct((b, t, SB_DIM), F32),
        jax.ShapeDtypeStruct((b, t // tv, SB_DIM, tv), BF16),
        jax.ShapeDtypeStruct((b, t, CONV_DIM), BF16),
        jax.ShapeDtypeStruct((b, CONV_HALO, CONV_DIM), F32),
    )
    return pl.pallas_call(
        functools.partial(_inproj_kernel, tm=tm, sub=sub, rows=rows),
        grid=(b, nt),
        in_specs=[
            row_spec(D_MODEL),
            _const_spec((1, D_MODEL)),
            _const_spec((D_MODEL, IN_DIM)),
            _const_spec((1, SB_DIM)),
            _const_spec((1, SB_DIM)),
            _const_spec((SB_DIM, SB_DIM)),
            _const_spec((CONV_WIDTH, CONV_DIM)),
            _const_spec((1, CONV_DIM)),
            _const_spec((1, CONV_DIM)),
            _const_spec((1, CONV_DIM)),
            pl.BlockSpec((1, CONV_HALO, CONV_DIM), lambda i, j: (i, 0, 0)),
        ],
        out_specs=(
            row_spec(SB_DIM), row_spec(SB_DIM), row_spec(SB_DIM), row_spec(SB_DIM),
            pl.BlockSpec((1, tm // tv, SB_DIM, tv), lambda i, j: (i, j, 0, 0)),
            row_spec(CONV_DIM),
            pl.BlockSpec((1, CONV_HALO, CONV_DIM), lambda i, j: (i, 0, 0)),
        ),
        out_shape=out_shape,
        scratch_shapes=[
            pltpu.VMEM((CONV_HALO + tm, CONV_DIM), F32),
            pltpu.VMEM((tm, CONV_DIM), F32),
        ],
        compiler_params=pltpu.CompilerParams(
            dimension_semantics=("arbitrary", "arbitrary"), vmem_limit_bytes=VMEM_LIMIT_BYTES),
        name="inproj_conv",
    )(x, p["g_mix"], p["w_in"], p["g_q"], p["g_k"], p["grp"], p["w_dw"], p["b_dw"],
      p["g_conv_ln"], p["b_conv_ln"], conv_state)


def _attn_kernel(stop_ref, q_ref, k_ref, vt_ref, o_ref, qh_ref, acc_ref, carry_ref, *, tq, tk, past, heads):
    q0 = past + pl.program_id(2) * tq
    n_full = q0 // tk

    for h in range(heads):
        l0 = (h // HEADS_PER_VREG) * LANES
        q2 = q_ref[0, :, l0:l0 + LANES]
        lane_head = lax.broadcasted_iota(jnp.int32, q2.shape, 1) // SB_HEAD_DIM
        qh_ref[h] = jnp.where(lane_head == h % HEADS_PER_VREG, q2, jnp.zeros_like(q2))
    suffix = (lax.broadcasted_iota(jnp.int32, (tk, tk), 1)
              >= lax.broadcasted_iota(jnp.int32, (tk, tk), 0)).astype(BF16)

    acc_ref[...] = jnp.zeros_like(acc_ref)
    carry_ref[...] = jnp.zeros_like(carry_ref)

    def key_tile(j, masked):
        rows = pl.ds(pl.multiple_of(j * tk, tk), tk)
        if masked:
            kpos = j * tk + lax.broadcasted_iota(jnp.int32, (tk, tq), 0)
            qpos = q0 + lax.broadcasted_iota(jnp.int32, (tk, tq), 1)
            visible = kpos < qpos
        zs, sps, withins, laters = [], [], [], []
        for h in range(heads):
            l0 = (h // HEADS_PER_VREG) * LANES
            zs.append(lax.dot_general(k_ref[0, rows, l0:l0 + LANES], qh_ref[h],
                                      (((1,), (1,)), ((), ())), preferred_element_type=F32))
        for z in zs:
            sp = jnp.maximum(z, 0.0) + jnp.log(1.0 + jnp.exp2(jnp.abs(z) * (-LOG2_E)))
            sps.append(jnp.where(visible, sp, 0.0) if masked else sp)
        for sp in sps:
            withins.append(jnp.dot(suffix, sp.astype(BF16), preferred_element_type=F32))
        for h in range(heads):
            laters.append(carry_ref[h:h + 1, :])
        carries = [laters[h] + withins[h][0:1, :] for h in range(heads)]
        more = jnp.min(functools.reduce(jnp.minimum, carries)) < stop_ref[0]
        for h in range(heads):
            a = jnp.exp(zs[h] - (withins[h] + laters[h]))
            if masked:
                a = jnp.where(visible, a, 0.0)
            hs = slice(h * SB_HEAD_DIM, (h + 1) * SB_HEAD_DIM)
            acc_ref[hs, :] += jnp.dot(vt_ref[0, j, hs, :], a.astype(BF16), preferred_element_type=F32)
            carry_ref[h:h + 1, :] = carries[h]
        return more

    def cond(state):
        i, more = state
        return jnp.logical_and(i < n_full, more)

    def body(state):
        i, _ = state
        return i + 1, key_tile(n_full - 1 - i, False)

    lax.while_loop(cond, body, (jnp.int32(0), key_tile(n_full, True)))
    o_ref[0] = acc_ref[...].T.astype(BF16)


def _attention(q, k_all, vt_all, stop, *, tq, tk, past, heads):
    b, t_q, _ = q.shape
    t_k = k_all.shape[1]
    nq = t_q // tq
    width = heads * SB_HEAD_DIM
    assert t_q % tq == 0 and past % tk == 0 and (tq == tk or nq == 1) and tq <= tk
    assert t_k % tk == 0 and t_k >= past + (nq - 1) * tq + tk
    assert heads % HEADS_PER_VREG == 0 and SB_HEADS % heads == 0
    return pl.pallas_call(
        functools.partial(_attn_kernel, tq=tq, tk=tk, past=past, heads=heads),
        grid=(b, SB_HEADS // heads, nq),
        in_specs=[
            pl.BlockSpec(memory_space=pltpu.SMEM),
            pl.BlockSpec((1, tq, width), lambda i, h, j: (i, j, h)),
            pl.BlockSpec((1, t_k, width), lambda i, h, j: (i, 0, h)),
            pl.BlockSpec((1, t_k // tk, width, tk), lambda i, h, j: (i, 0, h, 0)),
        ],
        out_specs=pl.BlockSpec((1, tq, width), lambda i, h, j: (i, j, h)),
        out_shape=jax.ShapeDtypeStruct((b, t_q, SB_DIM), BF16),
        scratch_shapes=[
            pltpu.VMEM((heads, tq, LANES), BF16),
            pltpu.VMEM((width, tq), F32),
            pltpu.VMEM((heads, tq), F32),
        ],
        compiler_params=pltpu.CompilerParams(
            dimension_semantics=("arbitrary", "arbitrary", "arbitrary"), vmem_limit_bytes=VMEM_LIMIT_BYTES),
        name="sb_attention",
    )(stop, q, k_all, vt_all)


def _ffn_kernel(x_ref, o_ref, cc_ref, wout_ref, gffn_ref, wup_ref, wfdw_ref, bfdw_ref, wdown_ref,
                fst_ref, y_ref, fso_ref, up_buf, act_buf, *, tm, d_ff, fc):
    @pl.when(pl.program_id(1) == 0)
    def _():
        up_buf[0:FFN_HALO, :] = fst_ref[0]

    h = (x_ref[0]
         + jnp.dot(o_ref[0], wout_ref[0:SB_DIM, :], preferred_element_type=F32)
         + jnp.dot(cc_ref[0], wout_ref[SB_DIM:, :], preferred_element_type=F32))
    ms = jnp.mean(h * h, axis=-1, keepdims=True)
    hn = (h * lax.rsqrt(ms + EPS) * gffn_ref[...]).astype(BF16)
    up_buf[FFN_HALO:FFN_HALO + tm, :] = jnp.dot(hn, wup_ref[...], preferred_element_type=F32)

    first = FFN_HALO - (FFN_CONV_WIDTH - 1)

    def conv(c0):
        win = up_buf[:, c0:c0 + fc]
        out = jnp.broadcast_to(bfdw_ref[:, c0:c0 + fc], (tm, fc))
        for j in range(FFN_CONV_WIDTH):
            shift = FFN_CONV_WIDTH - 1 - j
            shifted = win if shift == 0 else pltpu.roll(win, shift, axis=0)
            out = out + wfdw_ref[j:j + 1, c0:c0 + fc] * shifted[FFN_HALO:FFN_HALO + tm, :]
        return out

    for c0 in range(0, d_ff, fc):
        a = conv(c0)
        g = conv(d_ff + c0)
        act_buf[:, c0:c0 + fc] = (_swish(g) * a).astype(BF16)
    y_ref[0] = h + jnp.dot(act_buf[...], wdown_ref[...], preferred_element_type=F32)

    tail = up_buf[tm:tm + FFN_HALO, :]
    fso_ref[0] = tail
    up_buf[0:FFN_HALO, :] = tail


def _ffn(x, o, cc, ffn_state, p, *, tm):
    b, t, _ = x.shape
    d_ff = p["w_down"].shape[0]
    fc = MXU_WIDTH
    assert d_ff % fc == 0 and t % tm == 0
    row_spec = lambda w: pl.BlockSpec((1, tm, w), lambda i, j: (i, j, 0))
    return pl.pallas_call(
        functools.partial(_ffn_kernel, tm=tm, d_ff=d_ff, fc=fc),
        grid=(b, t // tm),
        in_specs=[
            row_spec(D_MODEL), row_spec(SB_DIM), row_spec(CONV_DIM),
            _const_spec((D_MODEL, D_MODEL)),
            _const_spec((1, D_MODEL)),
            _const_spec((D_MODEL, 2 * d_ff)),
            _const_spec((FFN_CONV_WIDTH, 2 * d_ff)),
            _const_spec((1, 2 * d_ff)),
            _const_spec((d_ff, D_MODEL)),
            pl.BlockSpec((1, FFN_HALO, 2 * d_ff), lambda i, j: (i, 0, 0)),
        ],
        out_specs=(
            row_spec(D_MODEL),
            pl.BlockSpec((1, FFN_HALO, 2 * d_ff), lambda i, j: (i, 0, 0)),
        ),
        out_shape=(
            jax.ShapeDtypeStruct((b, t, D_MODEL), F32),
            jax.ShapeDtypeStruct((b, FFN_HALO, 2 * d_ff), F32),
        ),
        scratch_shapes=[
            pltpu.VMEM((FFN_HALO + tm, 2 * d_ff), F32),
            pltpu.VMEM((tm, d_ff), BF16),
        ],
        compiler_params=pltpu.CompilerParams(
            dimension_semantics=("arbitrary", "arbitrary"), vmem_limit_bytes=VMEM_LIMIT_BYTES),
        name="outproj_convffn",
    )(x, o, cc, p["w_out"], p["g_ffn"], p["w_up"], p["w_ffn_dw"], p["b_ffn_dw"], p["w_down"], ffn_state)


def _pad_rows_front(a, n):
    return jnp.pad(a, ((0, 0), (n - a.shape[1], 0), (0, 0)))


KEY_TILE = 256
ROW_GROUP = 256
INPROJ_ROW_GROUPS = 2


def _tiling(t):
    if t % (ROW_GROUP * INPROJ_ROW_GROUPS) == 0:
        return ROW_GROUP * INPROJ_ROW_GROUPS, ROW_GROUP, KEY_TILE
    assert t % SUBLANES == 0 and CONV_WIDTH - 1 <= t <= LANES
    return t, t, LANES


def _layer(x, past_k, past_v, conv_state, ffn_state, p):
    b, t, _ = x.shape
    tm, sub, tq = _tiling(t)
    tk, heads = KEY_TILE, SB_HEADS
    q, kb, k, v, vt, cc, cso = _inproj(x, _pad_rows_front(conv_state, CONV_HALO), p, tm=tm, sub=sub)
    if past_k is None:
        past = 0
        stop = (BF16_ROUNDING_SLACK * p["q_norm_bound"] * p["k_norm_bound"] + F32_EXP_UNDERFLOW).reshape(1)
        o = _attention(q, kb, vt, stop, tq=tq, tk=tk, past=0, heads=heads)
    else:
        past = past_k.shape[1]
        t_pad = -(-t // tk) * tk
        k_all = jnp.pad(jnp.concatenate([past_k.astype(BF16), kb], axis=1), ((0, 0), (0, t_pad - t), (0, 0)))
        v_all = jnp.pad(jnp.concatenate([past_v.astype(BF16), v.astype(BF16)], axis=1), ((0, 0), (0, t_pad - t), (0, 0)))
        vt_all = v_all.reshape(b, (past + t_pad) // tk, tk, SB_DIM).swapaxes(2, 3)
        q_pad = jnp.pad(q, ((0, 0), (0, tq - t), (0, 0)))
        cached = past_k.reshape(b, past, SB_HEADS, SB_HEAD_DIM)
        k_norm = jnp.sqrt(jnp.max(jnp.sum(cached * cached, axis=-1)))
        stop = (BF16_ROUNDING_SLACK * p["q_norm_bound"] * jnp.maximum(k_norm, p["k_norm_bound"])
                + F32_EXP_UNDERFLOW).reshape(1)
        o = _attention(q_pad, k_all, vt_all, stop, tq=tq, tk=tk, past=past, heads=heads)[:, :t]
    y, fso = _ffn(x, o, cc, _pad_rows_front(ffn_state, FFN_HALO), p, tm=sub)
    return (y, k.reshape(b, t, SB_HEADS, SB_HEAD_DIM), v.reshape(b, t, SB_HEADS, SB_HEAD_DIM),
            cso[:, CONV_HALO - (CONV_WIDTH - 1):], fso[:, FFN_HALO - (FFN_CONV_WIDTH - 1):])


def kernel(x_prompt, x_sample, cache_sb_k, cache_sb_v, state_conv, state_ffn_conv, g_mix, w_in, g_q, g_k, w_dw, b_dw, g_conv_ln, b_conv_ln, w_out, g_ffn, w_up, w_ffn_dw, b_ffn_dw, w_down):
    depth = g_mix.shape[0]
    assert depth == 1
    bp = x_prompt.shape[0]
    bs, plen = cache_sb_k.shape[1:3]
    d_ff = w_down.shape[1]
    head_of = jnp.arange(SB_DIM) // SB_HEAD_DIM
    row = lambda a: a.reshape(1, -1)
    p = {
        "g_mix": row(g_mix[0]), "w_in": _to_bf16(w_in[0]),
        "g_q": row(jnp.tile(g_q[0], SB_HEADS)) * SB_HEAD_DIM ** -0.5, "g_k": row(jnp.tile(g_k[0], SB_HEADS)),
        "grp": ((head_of[:, None] == head_of[None, :]).astype(F32) / SB_HEAD_DIM).astype(BF16),
        "w_dw": w_dw[0], "b_dw": row(b_dw[0]),
        "g_conv_ln": row(g_conv_ln[0]), "b_conv_ln": row(b_conv_ln[0]),
        "w_out": _to_bf16(w_out[0]), "g_ffn": row(g_ffn[0]),
        "w_up": _to_bf16(w_up[0]), "w_ffn_dw": w_ffn_dw[0], "b_ffn_dw": row(b_ffn_dw[0]),
        "w_down": _to_bf16(w_down[0]),
        "q_norm_bound": jnp.max(jnp.abs(g_q[0])),
        "k_norm_bound": SB_HEAD_DIM ** 0.5 * jnp.max(jnp.abs(g_k[0])),
    }
    zc = jnp.zeros((bp, CONV_WIDTH - 1, CONV_DIM), F32)
    zf = jnp.zeros((bp, FFN_CONV_WIDTH - 1, 2 * d_ff), F32)
    yp, kp, vp, cp, fp = _layer(x_prompt, None, None, zc, zf, p)
    ys, ks, vs, cs, fs = _layer(
        x_sample, cache_sb_k[0].reshape(bs, plen, SB_DIM), cache_sb_v[0].reshape(bs, plen, SB_DIM),
        state_conv[0], state_ffn_conv[0], p)
    return (yp, ys, kp[None], vp[None], ks[None], vs[None], cp[None], cs[None], fp[None], fs[None])
```

```python
import functools

import jax
import jax.numpy as jnp
from jax import lax
from jax.experimental import pallas as pl
from jax.experimental.pallas import tpu as pltpu

D_MODEL = 1024
SB_HEADS = 8
SB_HEAD_DIM = 64
SB_DIM = SB_HEADS * SB_HEAD_DIM
CONV_DIM = D_MODEL - SB_DIM
CONV_WIDTH = 31
FFN_CONV_WIDTH = 3
IN_DIM = 3 * SB_DIM + 2 * CONV_DIM
EPS = 1e-6
LOG2_E = 1.4426950408889634
F32_EXP_UNDERFLOW = 110.0
BF16_ROUNDING_SLACK = 1.02

LANES = 128
SUBLANES = 8
MXU_WIDTH = 256
HEADS_PER_VREG = LANES // SB_HEAD_DIM
CONV_HALO = 32
FFN_HALO = 8
VMEM_LIMIT_BYTES = 56 * 1024 * 1024
CAST_BLOCK_ROWS = 256

F32 = jnp.float32
BF16 = jnp.bfloat16


def _sigmoid(x):
    return 0.5 * jnp.tanh(0.5 * x) + 0.5


def _swish(x):
    half = 0.5 * x
    return half * jnp.tanh(half) + half


def _const_spec(shape):
    return pl.BlockSpec(shape, lambda *_: (0,) * len(shape), pipeline_mode=pl.Buffered(1))


def _to_bf16_kernel(w_ref, o_ref):
    o_ref[...] = w_ref[...].astype(BF16)


def _to_bf16(w):
    rows, cols = w.shape
    block = min(rows, CAST_BLOCK_ROWS)
    assert rows % block == 0
    return pl.pallas_call(
        _to_bf16_kernel,
        grid=(rows // block,),
        in_specs=[pl.BlockSpec((block, cols), lambda i: (i, 0))],
        out_specs=pl.BlockSpec((block, cols), lambda i: (i, 0)),
        out_shape=jax.ShapeDtypeStruct((rows, cols), BF16),
        compiler_params=pltpu.CompilerParams(
            dimension_semantics=("arbitrary",), vmem_limit_bytes=VMEM_LIMIT_BYTES),
        name="weight_to_bf16",
    )(w)


def _inproj_kernel(x_ref, gmix_ref, win_ref, gq_ref, gk_ref, grp_ref, wdw_ref, bdw_ref, gln_ref,
                   bln_ref, cst_ref, q_ref, kb_ref, k_ref, v_ref, vt_ref, cc_ref, cso_ref,
                   cbuf, conv_buf, *, tm, sub, tv, rows):
    @pl.when(pl.program_id(1) == 0)
    def _():
        cbuf[0:CONV_HALO, :] = cst_ref[0]

    def project(m0):
        x = x_ref[0, m0:m0 + sub, :]
        ms = jnp.mean(x * x, axis=-1, keepdims=True)
        xn = (x * lax.rsqrt(ms + EPS) * gmix_ref[...]).astype(BF16)
        return jnp.dot(xn, win_ref[...], preferred_element_type=F32)

    def head_rmsnorm(a, g):
        msq = jnp.dot((a * a).astype(BF16), grp_ref[...], preferred_element_type=F32)
        return a * lax.rsqrt(msq + EPS) * g

    def finish(m0, proj):
        out_rows = slice(m0, m0 + sub)
        qn = head_rmsnorm(proj[:, 0:SB_DIM], gq_ref[...])
        kn = head_rmsnorm(proj[:, SB_DIM:2 * SB_DIM], gk_ref[...])
        v = proj[:, 2 * SB_DIM:3 * SB_DIM]
        q_ref[0, out_rows, :] = qn.astype(BF16)
        kb_ref[0, out_rows, :] = kn.astype(BF16)
        k_ref[0, out_rows, :] = kn
        v_ref[0, out_rows, :] = v
        vt_ref[0, m0 // tv, :, m0 % tv:m0 % tv + sub] = v.T.astype(BF16)

        u = proj[:, 3 * SB_DIM:3 * SB_DIM + CONV_DIM]
        gate = proj[:, 3 * SB_DIM + CONV_DIM:]
        cbuf[CONV_HALO + m0:CONV_HALO + m0 + sub, :] = u * _sigmoid(gate)

        first = CONV_HALO - (CONV_WIDTH - 1)
        n_win = rows + CONV_HALO
        for r0 in range(m0, m0 + sub, rows):
            for l0 in range(0, CONV_DIM, LANES):
                win = cbuf[r0:r0 + n_win, l0:l0 + LANES]
                acc = jnp.broadcast_to(bdw_ref[:, l0:l0 + LANES], (rows, LANES))
                for phase in range(SUBLANES):
                    shifted = win if phase == 0 else pltpu.roll(win, n_win - phase, axis=0)
                    for base in range(0, CONV_HALO + 1, SUBLANES):
                        j = base + phase - first
                        if 0 <= j < CONV_WIDTH:
                            acc = acc + wdw_ref[j:j + 1, l0:l0 + LANES] * shifted[base:base + rows, :]
                conv_buf[r0:r0 + rows, l0:l0 + LANES] = acc

        c = conv_buf[out_rows, :]
        mu = jnp.mean(c, axis=-1, keepdims=True)
        xc = c - mu
        var = jnp.mean(xc * xc, axis=-1, keepdims=True)
        y = xc * lax.rsqrt(var + EPS) * gln_ref[...] + bln_ref[...]
        cc_ref[0, out_rows, :] = _swish(y).astype(BF16)

    starts = list(range(0, tm, sub))
    proj = project(starts[0])
    for i, m0 in enumerate(starts):
        nxt = project(starts[i + 1]) if i + 1 < len(starts) else None
        finish(m0, proj)
        proj = nxt

    tail = cbuf[tm:tm + CONV_HALO, :]
    cso_ref[0] = tail
    cbuf[0:CONV_HALO, :] = tail


def _inproj(x, conv_state, p, *, tm, sub, tv):
    b, t, _ = x.shape
    nt = t // tm
    rows = min(sub, 64)
    assert tm % tv == 0 and tv % sub == 0
    row_spec = lambda w: pl.BlockSpec((1, tm, w), lambda i, j: (i, j, 0))
    out_shape = (
        jax.ShapeDtypeStruct((b, t, SB_DIM), BF16),
        jax.ShapeDtypeStruct((b, t, SB_DIM), BF16),
        jax.ShapeDtypeStruct((b, t, SB_DIM), F32),
        jax.ShapeDtypeStruct((b, t, SB_DIM), F32),
        jax.ShapeDtypeStruct((b, t // tv, SB_DIM, tv), BF16),
        jax.ShapeDtypeStruct((b, t, CONV_DIM), BF16),
        jax.ShapeDtypeStruct((b, CONV_HALO, CONV_DIM), F32),
    )
    return pl.pallas_call(
        functools.partial(_inproj_kernel, tm=tm, sub=sub, tv=tv, rows=rows),
        grid=(b, nt),
        in_specs=[
            row_spec(D_MODEL),
            _const_spec((1, D_MODEL)),
            _const_spec((D_MODEL, IN_DIM)),
            _const_spec((1, SB_DIM)),
            _const_spec((1, SB_DIM)),
            _const_spec((SB_DIM, SB_DIM)),
            _const_spec((CONV_WIDTH, CONV_DIM)),
            _const_spec((1, CONV_DIM)),
            _const_spec((1, CONV_DIM)),
            _const_spec((1, CONV_DIM)),
            pl.BlockSpec((1, CONV_HALO, CONV_DIM), lambda i, j: (i, 0, 0)),
        ],
        out_specs=(
            row_spec(SB_DIM), row_spec(SB_DIM), row_spec(SB_DIM), row_spec(SB_DIM),
            pl.BlockSpec((1, tm // tv, SB_DIM, tv), lambda i, j: (i, j, 0, 0)),
            row_spec(CONV_DIM),
            pl.BlockSpec((1, CONV_HALO, CONV_DIM), lambda i, j: (i, 0, 0)),
        ),
        out_shape=out_shape,
        scratch_shapes=[
            pltpu.VMEM((CONV_HALO + tm, CONV_DIM), F32),
            pltpu.VMEM((tm, CONV_DIM), F32),
        ],
        compiler_params=pltpu.CompilerParams(
            dimension_semantics=("arbitrary", "arbitrary"), vmem_limit_bytes=VMEM_LIMIT_BYTES),
        name="inproj_conv",
    )(x, p["g_mix"], p["w_in"], p["g_q"], p["g_k"], p["grp"], p["w_dw"], p["b_dw"],
      p["g_conv_ln"], p["b_conv_ln"], conv_state)


def _attn_kernel(stop_ref, q_ref, k_ref, vt_ref, o_ref, qh_ref, acc_ref, carry_ref, *, tq, tk, past, heads):
    q0 = past + pl.program_id(2) * tq
    n_full = q0 // tk

    for h in range(heads):
        l0 = (h // HEADS_PER_VREG) * LANES
        q2 = q_ref[0, :, l0:l0 + LANES]
        lane_head = lax.broadcasted_iota(jnp.int32, q2.shape, 1) // SB_HEAD_DIM
        qh_ref[h] = jnp.where(lane_head == h % HEADS_PER_VREG, q2, jnp.zeros_like(q2))
    suffix = (lax.broadcasted_iota(jnp.int32, (tk, tk), 1)
              >= lax.broadcasted_iota(jnp.int32, (tk, tk), 0)).astype(BF16)

    acc_ref[...] = jnp.zeros_like(acc_ref)
    carry_ref[...] = jnp.zeros_like(carry_ref)

    def key_tile(j, masked):
        rows = pl.ds(pl.multiple_of(j * tk, tk), tk)
        if masked:
            kpos = j * tk + lax.broadcasted_iota(jnp.int32, (tk, tq), 0)
            qpos = q0 + lax.broadcasted_iota(jnp.int32, (tk, tq), 1)
            visible = kpos < qpos
        zs, sps, withins, laters = [], [], [], []
        for h in range(heads):
            l0 = (h // HEADS_PER_VREG) * LANES
            zs.append(lax.dot_general(k_ref[0, rows, l0:l0 + LANES], qh_ref[h],
                                      (((1,), (1,)), ((), ())), preferred_element_type=F32))
        for z in zs:
            sp = jnp.maximum(z, 0.0) + jnp.log(1.0 + jnp.exp2(jnp.abs(z) * (-LOG2_E)))
            sps.append(jnp.where(visible, sp, 0.0) if masked else sp)
        for sp in sps:
            withins.append(jnp.dot(suffix, sp.astype(BF16), preferred_element_type=F32))
        for h in range(heads):
            laters.append(carry_ref[h:h + 1, :])
        carries = [laters[h] + withins[h][0:1, :] for h in range(heads)]
        more = jnp.min(functools.reduce(jnp.minimum, carries)) < stop_ref[0]
        for h in range(heads):
            a = jnp.exp(zs[h] - (withins[h] + laters[h]))
            if masked:
                a = jnp.where(visible, a, 0.0)
            hs = slice(h * SB_HEAD_DIM, (h + 1) * SB_HEAD_DIM)
            acc_ref[hs, :] += jnp.dot(vt_ref[0, j, hs, :], a.astype(BF16), preferred_element_type=F32)
            carry_ref[h:h + 1, :] = carries[h]
        return more

    def cond(state):
        i, more = state
        return jnp.logical_and(i < n_full, more)

    def body(state):
        i, _ = state
        return i + 1, key_tile(n_full - 1 - i, False)

    lax.while_loop(cond, body, (jnp.int32(0), key_tile(n_full, True)))
    o_ref[0] = acc_ref[...].T.astype(BF16)


def _attention(q, k_all, vt_all, stop, *, tq, tk, past, heads):
    b, t_q, _ = q.shape
    t_k = k_all.shape[1]
    nq = t_q // tq
    width = heads * SB_HEAD_DIM
    assert t_q % tq == 0 and past % tk == 0 and (tq == tk or nq == 1) and tq <= tk
    assert t_k % tk == 0 and t_k >= past + (nq - 1) * tq + tk
    assert heads % HEADS_PER_VREG == 0 and SB_HEADS % heads == 0
    return pl.pallas_call(
        functools.partial(_attn_kernel, tq=tq, tk=tk, past=past, heads=heads),
        grid=(b, SB_HEADS // heads, nq),
        in_specs=[
            pl.BlockSpec(memory_space=pltpu.SMEM),
            pl.BlockSpec((1, tq, width), lambda i, h, j: (i, j, h)),
            pl.BlockSpec((1, t_k, width), lambda i, h, j: (i, 0, h)),
            pl.BlockSpec((1, t_k // tk, width, tk), lambda i, h, j: (i, 0, h, 0)),
        ],
        out_specs=pl.BlockSpec((1, tq, width), lambda i, h, j: (i, j, h)),
        out_shape=jax.ShapeDtypeStruct((b, t_q, SB_DIM), BF16),
        scratch_shapes=[
            pltpu.VMEM((heads, tq, LANES), BF16),
            pltpu.VMEM((width, tq), F32),
            pltpu.VMEM((heads, tq), F32),
        ],
        compiler_params=pltpu.CompilerParams(
            dimension_semantics=("arbitrary", "arbitrary", "arbitrary"), vmem_limit_bytes=VMEM_LIMIT_BYTES),
        name="sb_attention",
    )(stop, q, k_all, vt_all)


def _ffn_kernel(x_ref, o_ref, cc_ref, wout_ref, gffn_ref, wup_ref, wfdw_ref, bfdw_ref, wdown_ref,
                fst_ref, y_ref, fso_ref, up_buf, act_buf, *, tm, d_ff, fc):
    @pl.when(pl.program_id(1) == 0)
    def _():
        up_buf[0:FFN_HALO, :] = fst_ref[0]

    h = (x_ref[0]
         + jnp.dot(o_ref[0], wout_ref[0:SB_DIM, :], preferred_element_type=F32)
         + jnp.dot(cc_ref[0], wout_ref[SB_DIM:, :], preferred_element_type=F32))
    ms = jnp.mean(h * h, axis=-1, keepdims=True)
    hn = (h * lax.rsqrt(ms + EPS) * gffn_ref[...]).astype(BF16)
    up_buf[FFN_HALO:FFN_HALO + tm, :] = jnp.dot(hn, wup_ref[...], preferred_element_type=F32)

    first = FFN_HALO - (FFN_CONV_WIDTH - 1)

    def conv(c0):
        win = up_buf[:, c0:c0 + fc]
        out = jnp.broadcast_to(bfdw_ref[:, c0:c0 + fc], (tm, fc))
        for j in range(FFN_CONV_WIDTH):
            shift = FFN_CONV_WIDTH - 1 - j
            shifted = win if shift == 0 else pltpu.roll(win, shift, axis=0)
            out = out + wfdw_ref[j:j + 1, c0:c0 + fc] * shifted[FFN_HALO:FFN_HALO + tm, :]
        return out

    for c0 in range(0, d_ff, fc):
        a = conv(c0)
        g = conv(d_ff + c0)
        act_buf[:, c0:c0 + fc] = (_swish(g) * a).astype(BF16)
    y_ref[0] = h + jnp.dot(act_buf[...], wdown_ref[...], preferred_element_type=F32)

    tail = up_buf[tm:tm + FFN_HALO, :]
    fso_ref[0] = tail
    up_buf[0:FFN_HALO, :] = tail


def _ffn(x, o, cc, ffn_state, p, *, tm):
    b, t, _ = x.shape
    d_ff = p["w_down"].shape[0]
    fc = MXU_WIDTH
    assert d_ff % fc == 0 and t % tm == 0
    row_spec = lambda w: pl.BlockSpec((1, tm, w), lambda i, j: (i, j, 0))
    return pl.pallas_call(
        functools.partial(_ffn_kernel, tm=tm, d_ff=d_ff, fc=fc),
        grid=(b, t // tm),
        in_specs=[
            row_spec(D_MODEL), row_spec(SB_DIM), row_spec(CONV_DIM),
            _const_spec((D_MODEL, D_MODEL)),
            _const_spec((1, D_MODEL)),
            _const_spec((D_MODEL, 2 * d_ff)),
            _const_spec((FFN_CONV_WIDTH, 2 * d_ff)),
            _const_spec((1, 2 * d_ff)),
            _const_spec((d_ff, D_MODEL)),
            pl.BlockSpec((1, FFN_HALO, 2 * d_ff), lambda i, j: (i, 0, 0)),
        ],
        out_specs=(
            row_spec(D_MODEL),
            pl.BlockSpec((1, FFN_HALO, 2 * d_ff), lambda i, j: (i, 0, 0)),
        ),
        out_shape=(
            jax.ShapeDtypeStruct((b, t, D_MODEL), F32),
            jax.ShapeDtypeStruct((b, FFN_HALO, 2 * d_ff), F32),
        ),
        scratch_shapes=[
            pltpu.VMEM((FFN_HALO + tm, 2 * d_ff), F32),
            pltpu.VMEM((tm, d_ff), BF16),
        ],
        compiler_params=pltpu.CompilerParams(
            dimension_semantics=("arbitrary", "arbitrary"), vmem_limit_bytes=VMEM_LIMIT_BYTES),
        name="outproj_convffn",
    )(x, o, cc, p["w_out"], p["g_ffn"], p["w_up"], p["w_ffn_dw"], p["b_ffn_dw"], p["w_down"], ffn_state)


def _pad_rows_front(a, n):
    return jnp.pad(a, ((0, 0), (n - a.shape[1], 0), (0, 0)))


KEY_TILE = 256
INPROJ_TILE = 512
INPROJ_ROW_GROUP = 128
FFN_TILE = 512


def _tiling(t):
    if t % INPROJ_TILE == 0:
        return INPROJ_TILE, INPROJ_ROW_GROUP, KEY_TILE, FFN_TILE, KEY_TILE
    assert t % SUBLANES == 0 and CONV_WIDTH - 1 <= t <= LANES
    return t, t, t, t, LANES


def _layer(x, past_k, past_v, conv_state, ffn_state, p):
    b, t, _ = x.shape
    tm, sub, tv, ffn_tm, tq = _tiling(t)
    tk, heads = KEY_TILE, SB_HEADS
    q, kb, k, v, vt, cc, cso = _inproj(x, _pad_rows_front(conv_state, CONV_HALO), p, tm=tm, sub=sub, tv=tv)
    if past_k is None:
        past = 0
        stop = (BF16_ROUNDING_SLACK * p["q_norm_bound"] * p["k_norm_bound"] + F32_EXP_UNDERFLOW).reshape(1)
        o = _attention(q, kb, vt, stop, tq=tq, tk=tk, past=0, heads=heads)
    else:
        past = past_k.shape[1]
        t_pad = -(-t // tk) * tk
        k_all = jnp.pad(jnp.concatenate([past_k.astype(BF16), kb], axis=1), ((0, 0), (0, t_pad - t), (0, 0)))
        v_all = jnp.pad(jnp.concatenate([past_v.astype(BF16), v.astype(BF16)], axis=1), ((0, 0), (0, t_pad - t), (0, 0)))
        vt_all = v_all.reshape(b, (past + t_pad) // tk, tk, SB_DIM).swapaxes(2, 3)
        q_pad = jnp.pad(q, ((0, 0), (0, tq - t), (0, 0)))
        keys = k_all.astype(F32).reshape(b, past + t_pad, SB_HEADS, SB_HEAD_DIM)
        k_norm = jnp.sqrt(jnp.max(jnp.sum(keys * keys, axis=-1)))
        stop = (BF16_ROUNDING_SLACK * p["q_norm_bound"] * k_norm + F32_EXP_UNDERFLOW).reshape(1)
        o = _attention(q_pad, k_all, vt_all, stop, tq=tq, tk=tk, past=past, heads=heads)[:, :t]
    y, fso = _ffn(x, o, cc, _pad_rows_front(ffn_state, FFN_HALO), p, tm=ffn_tm)
    return (y, k.reshape(b, t, SB_HEADS, SB_HEAD_DIM), v.reshape(b, t, SB_HEADS, SB_HEAD_DIM),
            cso[:, CONV_HALO - (CONV_WIDTH - 1):], fso[:, FFN_HALO - (FFN_CONV_WIDTH - 1):])


def kernel(x_prompt, x_sample, cache_sb_k, cache_sb_v, state_conv, state_ffn_conv, g_mix, w_in, g_q, g_k, w_dw, b_dw, g_conv_ln, b_conv_ln, w_out, g_ffn, w_up, w_ffn_dw, b_ffn_dw, w_down):
    depth = g_mix.shape[0]
    assert depth == 1
    bp = x_prompt.shape[0]
    bs, plen = cache_sb_k.shape[1:3]
    d_ff = w_down.shape[1]
    head_of = jnp.arange(SB_DIM) // SB_HEAD_DIM
    row = lambda a: a.reshape(1, -1)
    p = {
        "g_mix": row(g_mix[0]), "w_in": _to_bf16(w_in[0]),
        "g_q": row(jnp.tile(g_q[0], SB_HEADS)) * SB_HEAD_DIM ** -0.5, "g_k": row(jnp.tile(g_k[0], SB_HEADS)),
        "grp": ((head_of[:, None] == head_of[None, :]).astype(F32) / SB_HEAD_DIM).astype(BF16),
        "w_dw": w_dw[0], "b_dw": row(b_dw[0]),
        "g_conv_ln": row(g_conv_ln[0]), "b_conv_ln": row(b_conv_ln[0]),
        "w_out": _to_bf16(w_out[0]), "g_ffn": row(g_ffn[0]),
        "w_up": _to_bf16(w_up[0]), "w_ffn_dw": w_ffn_dw[0], "b_ffn_dw": row(b_ffn_dw[0]),
        "w_down": _to_bf16(w_down[0]),
        "q_norm_bound": jnp.max(jnp.abs(g_q[0])),
        "k_norm_bound": SB_HEAD_DIM ** 0.5 * jnp.max(jnp.abs(g_k[0])),
    }
    zc = jnp.zeros((bp, CONV_WIDTH - 1, CONV_DIM), F32)
    zf = jnp.zeros((bp, FFN_CONV_WIDTH - 1, 2 * d_ff), F32)
    yp, kp, vp, cp, fp = _layer(x_prompt, None, None, zc, zf, p)
    ys, ks, vs, cs, fs = _layer(
        x_sample, cache_sb_k[0].reshape(bs, plen, SB_DIM), cache_sb_v[0].reshape(bs, plen, SB_DIM),
        state_conv[0], state_ffn_conv[0], p)
    return (yp, ys, kp[None], vp[None], ks[None], vs[None], cp[None], cs[None], fp[None], fs[None])
```

```python
import functools

import jax
import jax.numpy as jnp
from jax import lax
from jax.experimental import pallas as pl
from jax.experimental.pallas import tpu as pltpu

D_MODEL = 1024
SB_HEADS = 8
SB_HEAD_DIM = 64
SB_DIM = SB_HEADS * SB_HEAD_DIM
CONV_DIM = D_MODEL - SB_DIM
CONV_WIDTH = 31
FFN_CONV_WIDTH = 3
IN_DIM = 3 * SB_DIM + 2 * CONV_DIM
EPS = 1e-6
LOG2_E = 1.4426950408889634
F32_EXP_UNDERFLOW = 110.0
BF16_ROUNDING_SLACK = 1.02

LANES = 128
SUBLANES = 8
MXU_WIDTH = 256
HEADS_PER_VREG = LANES // SB_HEAD_DIM
CONV_HALO = 32
FFN_HALO = 8
VMEM_LIMIT_BYTES = 56 * 1024 * 1024
CAST_BLOCK_ROWS = 256

F32 = jnp.float32
BF16 = jnp.bfloat16


def _sigmoid(x):
    return 0.5 * jnp.tanh(0.5 * x) + 0.5


def _swish(x):
    half = 0.5 * x
    return half * jnp.tanh(half) + half


def _const_spec(shape):
    return pl.BlockSpec(shape, lambda *_: (0,) * len(shape), pipeline_mode=pl.Buffered(1))


def _to_bf16_kernel(w_ref, o_ref):
    o_ref[...] = w_ref[...].astype(BF16)


def _to_bf16(w):
    rows, cols = w.shape
    block = min(rows, CAST_BLOCK_ROWS)
    assert rows % block == 0
    return pl.pallas_call(
        _to_bf16_kernel,
        grid=(rows // block,),
        in_specs=[pl.BlockSpec((block, cols), lambda i: (i, 0))],
        out_specs=pl.BlockSpec((block, cols), lambda i: (i, 0)),
        out_shape=jax.ShapeDtypeStruct((rows, cols), BF16),
        compiler_params=pltpu.CompilerParams(
            dimension_semantics=("arbitrary",), vmem_limit_bytes=VMEM_LIMIT_BYTES),
        name="weight_to_bf16",
    )(w)


def _inproj_kernel(x_ref, gmix_ref, win_ref, gq_ref, gk_ref, grp_ref, wdw_ref, bdw_ref, gln_ref,
                   bln_ref, cst_ref, q_ref, kb_ref, k_ref, v_ref, vt_ref, cc_ref, cso_ref,
                   cbuf, conv_buf, *, tm, sub, tv, rows):
    @pl.when(pl.program_id(1) == 0)
    def _():
        cbuf[0:CONV_HALO, :] = cst_ref[0]

    def project(m0):
        x = x_ref[0, m0:m0 + sub, :]
        ms = jnp.mean(x * x, axis=-1, keepdims=True)
        xn = (x * lax.rsqrt(ms + EPS) * gmix_ref[...]).astype(BF16)
        return jnp.dot(xn, win_ref[...], preferred_element_type=F32)

    def head_rmsnorm(a, g):
        msq = jnp.dot((a * a).astype(BF16), grp_ref[...], preferred_element_type=F32)
        return a * lax.rsqrt(msq + EPS) * g

    def finish(m0, proj):
        out_rows = slice(m0, m0 + sub)
        qn = head_rmsnorm(proj[:, 0:SB_DIM], gq_ref[...])
        kn = head_rmsnorm(proj[:, SB_DIM:2 * SB_DIM], gk_ref[...])
        v = proj[:, 2 * SB_DIM:3 * SB_DIM]
        q_ref[0, out_rows, :] = qn.astype(BF16)
        kb_ref[0, out_rows, :] = kn.astype(BF16)
        k_ref[0, out_rows, :] = kn
        v_ref[0, out_rows, :] = v
        vt_ref[0, m0 // tv, :, m0 % tv:m0 % tv + sub] = v.T.astype(BF16)

        u = proj[:, 3 * SB_DIM:3 * SB_DIM + CONV_DIM]
        gate = proj[:, 3 * SB_DIM + CONV_DIM:]
        cbuf[CONV_HALO + m0:CONV_HALO + m0 + sub, :] = u * _sigmoid(gate)

        first = CONV_HALO - (CONV_WIDTH - 1)
        n_win = rows + CONV_HALO
        for r0 in range(m0, m0 + sub, rows):
            for l0 in range(0, CONV_DIM, LANES):
                win = cbuf[r0:r0 + n_win, l0:l0 + LANES]
                acc = jnp.broadcast_to(bdw_ref[:, l0:l0 + LANES], (rows, LANES))
                for phase in range(SUBLANES):
                    shifted = win if phase == 0 else pltpu.roll(win, n_win - phase, axis=0)
                    for base in range(0, CONV_HALO + 1, SUBLANES):
                        j = base + phase - first
                        if 0 <= j < CONV_WIDTH:
                            acc = acc + wdw_ref[j:j + 1, l0:l0 + LANES] * shifted[base:base + rows, :]
                conv_buf[r0:r0 + rows, l0:l0 + LANES] = acc

        c = conv_buf[out_rows, :]
        mu = jnp.mean(c, axis=-1, keepdims=True)
        xc = c - mu
        var = jnp.mean(xc * xc, axis=-1, keepdims=True)
        y = xc * lax.rsqrt(var + EPS) * gln_ref[...] + bln_ref[...]
        cc_ref[0, out_rows, :] = _swish(y).astype(BF16)

    starts = list(range(0, tm, sub))
    proj = project(starts[0])
    for i, m0 in enumerate(starts):
        nxt = project(starts[i + 1]) if i + 1 < len(starts) else None
        finish(m0, proj)
        proj = nxt

    tail = cbuf[tm:tm + CONV_HALO, :]
    cso_ref[0] = tail
    cbuf[0:CONV_HALO, :] = tail


def _inproj(x, conv_state, p, *, tm, sub, tv):
    b, t, _ = x.shape
    nt = t // tm
    rows = min(sub, 64)
    assert tm % tv == 0 and tv % sub == 0
    row_spec = lambda w: pl.BlockSpec((1, tm, w), lambda i, j: (i, j, 0))
    out_shape = (
        jax.ShapeDtypeStruct((b, t, SB_DIM), BF16),
        jax.ShapeDtypeStruct((b, t, SB_DIM), BF16),
        jax.ShapeDtypeStruct((b, t, SB_DIM), F32),
        jax.ShapeDtypeStruct((b, t, SB_DIM), F32),
        jax.ShapeDtypeStruct((b, t // tv, SB_DIM, tv), BF16),
        jax.ShapeDtypeStruct((b, t, CONV_DIM), BF16),
        jax.ShapeDtypeStruct((b, CONV_HALO, CONV_DIM), F32),
    )
    return pl.pallas_call(
        functools.partial(_inproj_kernel, tm=tm, sub=sub, tv=tv, rows=rows),
        grid=(b, nt),
        in_specs=[
            row_spec(D_MODEL),
            _const_spec((1, D_MODEL)),
            _const_spec((D_MODEL, IN_DIM)),
            _const_spec((1, SB_DIM)),
            _const_spec((1, SB_DIM)),
            _const_spec((SB_DIM, SB_DIM)),
            _const_spec((CONV_WIDTH, CONV_DIM)),
            _const_spec((1, CONV_DIM)),
            _const_spec((1, CONV_DIM)),
            _const_spec((1, CONV_DIM)),
            pl.BlockSpec((1, CONV_HALO, CONV_DIM), lambda i, j: (i, 0, 0)),
        ],
        out_specs=(
            row_spec(SB_DIM), row_spec(SB_DIM), row_spec(SB_DIM), row_spec(SB_DIM),
            pl.BlockSpec((1, tm // tv, SB_DIM, tv), lambda i, j: (i, j, 0, 0)),
            row_spec(CONV_DIM),
            pl.BlockSpec((1, CONV_HALO, CONV_DIM), lambda i, j: (i, 0, 0)),
        ),
        out_shape=out_shape,
        scratch_shapes=[
            pltpu.VMEM((CONV_HALO + tm, CONV_DIM), F32),
            pltpu.VMEM((tm, CONV_DIM), F32),
        ],
        compiler_params=pltpu.CompilerParams(
            dimension_semantics=("arbitrary", "arbitrary"), vmem_limit_bytes=VMEM_LIMIT_BYTES),
        name="inproj_conv",
    )(x, p["g_mix"], p["w_in"], p["g_q"], p["g_k"], p["grp"], p["w_dw"], p["b_dw"],
      p["g_conv_ln"], p["b_conv_ln"], conv_state)


def _attn_kernel(stop_ref, q_ref, k_ref, vt_ref, o_ref, qh_ref, acc_ref, carry_ref, *, tq, tk, past, heads):
    q0 = past + pl.program_id(2) * tq
    n_full = q0 // tk

    for h in range(heads):
        l0 = (h // HEADS_PER_VREG) * LANES
        q2 = q_ref[0, :, l0:l0 + LANES]
        lane_head = lax.broadcasted_iota(jnp.int32, q2.shape, 1) // SB_HEAD_DIM
        qh_ref[h] = jnp.where(lane_head == h % HEADS_PER_VREG, q2, jnp.zeros_like(q2))
    suffix = (lax.broadcasted_iota(jnp.int32, (tk, tk), 1)
              >= lax.broadcasted_iota(jnp.int32, (tk, tk), 0)).astype(BF16)

    acc_ref[...] = jnp.zeros_like(acc_ref)
    carry_ref[...] = jnp.zeros_like(carry_ref)

    def key_tile(j, masked):
        rows = pl.ds(pl.multiple_of(j * tk, tk), tk)
        if masked:
            kpos = j * tk + lax.broadcasted_iota(jnp.int32, (tk, tq), 0)
            qpos = q0 + lax.broadcasted_iota(jnp.int32, (tk, tq), 1)
            visible = kpos < qpos
        zs, sps, withins, laters = [], [], [], []
        for h in range(heads):
            l0 = (h // HEADS_PER_VREG) * LANES
            zs.append(lax.dot_general(k_ref[0, rows, l0:l0 + LANES], qh_ref[h],
                                      (((1,), (1,)), ((), ())), preferred_element_type=F32))
        for z in zs:
            sp = jnp.maximum(z, 0.0) + jnp.log(1.0 + jnp.exp2(jnp.abs(z) * (-LOG2_E)))
            sps.append(jnp.where(visible, sp, 0.0) if masked else sp)
        for sp in sps:
            withins.append(jnp.dot(suffix, sp.astype(BF16), preferred_element_type=F32))
        for h in range(heads):
            laters.append(carry_ref[h:h + 1, :])
        carries = [laters[h] + withins[h][0:1, :] for h in range(heads)]
        more = jnp.min(functools.reduce(jnp.minimum, carries)) < stop_ref[0]
        for h in range(heads):
            a = jnp.exp(zs[h] - (withins[h] + laters[h]))
            if masked:
                a = jnp.where(visible, a, 0.0)
            hs = slice(h * SB_HEAD_DIM, (h + 1) * SB_HEAD_DIM)
            acc_ref[hs, :] += jnp.dot(vt_ref[0, j, hs, :], a.astype(BF16), preferred_element_type=F32)
            carry_ref[h:h + 1, :] = carries[h]
        return more

    def cond(state):
        i, more = state
        return jnp.logical_and(i < n_full, more)

    def body(state):
        i, _ = state
        return i + 1, key_tile(n_full - 1 - i, False)

    lax.while_loop(cond, body, (jnp.int32(0), key_tile(n_full, True)))
    o_ref[0] = acc_ref[...].T.astype(BF16)


def _attention(q, k_all, vt_all, stop, *, tq, tk, past, heads):
    b, t_q, _ = q.shape
    t_k = k_all.shape[1]
    nq = t_q // tq
    width = heads * SB_HEAD_DIM
    assert t_q % tq == 0 and past % tk == 0 and (tq == tk or nq == 1) and tq <= tk
    assert t_k % tk == 0 and t_k >= past + (nq - 1) * tq + tk
    assert heads % HEADS_PER_VREG == 0 and SB_HEADS % heads == 0
    return pl.pallas_call(
        functools.partial(_attn_kernel, tq=tq, tk=tk, past=past, heads=heads),
        grid=(b, SB_HEADS // heads, nq),
        in_specs=[
            pl.BlockSpec(memory_space=pltpu.SMEM),
            pl.BlockSpec((1, tq, width), lambda i, h, j: (i, j, h)),
            pl.BlockSpec((1, t_k, width), lambda i, h, j: (i, 0, h)),
            pl.BlockSpec((1, t_k // tk, width, tk), lambda i, h, j: (i, 0, h, 0)),
        ],
        out_specs=pl.BlockSpec((1, tq, width), lambda i, h, j: (i, j, h)),
        out_shape=jax.ShapeDtypeStruct((b, t_q, SB_DIM), BF16),
        scratch_shapes=[
            pltpu.VMEM((heads, tq, LANES), BF16),
            pltpu.VMEM((width, tq), F32),
            pltpu.VMEM((heads, tq), F32),
        ],
        compiler_params=pltpu.CompilerParams(
            dimension_semantics=("arbitrary", "arbitrary", "arbitrary"), vmem_limit_bytes=VMEM_LIMIT_BYTES),
        name="sb_attention",
    )(stop, q, k_all, vt_all)


def _ffn_kernel(x_ref, o_ref, cc_ref, wout_ref, gffn_ref, wup_ref, wfdw_ref, bfdw_ref, wdown_ref,
                fst_ref, y_ref, fso_ref, up_buf, act_buf, *, tm, d_ff, fc):
    @pl.when(pl.program_id(1) == 0)
    def _():
        up_buf[0:FFN_HALO, :] = fst_ref[0]

    h = (x_ref[0]
         + jnp.dot(o_ref[0], wout_ref[0:SB_DIM, :], preferred_element_type=F32)
         + jnp.dot(cc_ref[0], wout_ref[SB_DIM:, :], preferred_element_type=F32))
    ms = jnp.mean(h * h, axis=-1, keepdims=True)
    hn = (h * lax.rsqrt(ms + EPS) * gffn_ref[...]).astype(BF16)
    up_buf[FFN_HALO:FFN_HALO + tm, :] = jnp.dot(hn, wup_ref[...], preferred_element_type=F32)

    first = FFN_HALO - (FFN_CONV_WIDTH - 1)

    def conv(c0):
        win = up_buf[:, c0:c0 + fc]
        out = jnp.broadcast_to(bfdw_ref[:, c0:c0 + fc], (tm, fc))
        for j in range(FFN_CONV_WIDTH):
            shift = FFN_CONV_WIDTH - 1 - j
            shifted = win if shift == 0 else pltpu.roll(win, shift, axis=0)
            out = out + wfdw_ref[j:j + 1, c0:c0 + fc] * shifted[FFN_HALO:FFN_HALO + tm, :]
        return out

    for c0 in range(0, d_ff, fc):
        a = conv(c0)
        g = conv(d_ff + c0)
        act_buf[:, c0:c0 + fc] = (_swish(g) * a).astype(BF16)
    y_ref[0] = h + jnp.dot(act_buf[...], wdown_ref[...], preferred_element_type=F32)

    tail = up_buf[tm:tm + FFN_HALO, :]
    fso_ref[0] = tail
    up_buf[0:FFN_HALO, :] = tail


def _ffn(x, o, cc, ffn_state, p, *, tm):
    b, t, _ = x.shape
    d_ff = p["w_down"].shape[0]
    fc = MXU_WIDTH
    assert d_ff % fc == 0 and t % tm == 0
    row_spec = lambda w: pl.BlockSpec((1, tm, w), lambda i, j: (i, j, 0))
    return pl.pallas_call(
        functools.partial(_ffn_kernel, tm=tm, d_ff=d_ff, fc=fc),
        grid=(b, t // tm),
        in_specs=[
            row_spec(D_MODEL), row_spec(SB_DIM), row_spec(CONV_DIM),
            _const_spec((D_MODEL, D_MODEL)),
            _const_spec((1, D_MODEL)),
            _const_spec((D_MODEL, 2 * d_ff)),
            _const_spec((FFN_CONV_WIDTH, 2 * d_ff)),
            _const_spec((1, 2 * d_ff)),
            _const_spec((d_ff, D_MODEL)),
            pl.BlockSpec((1, FFN_HALO, 2 * d_ff), lambda i, j: (i, 0, 0)),
        ],
        out_specs=(
            row_spec(D_MODEL),
            pl.BlockSpec((1, FFN_HALO, 2 * d_ff), lambda i, j: (i, 0, 0)),
        ),
        out_shape=(
            jax.ShapeDtypeStruct((b, t, D_MODEL), F32),
            jax.ShapeDtypeStruct((b, FFN_HALO, 2 * d_ff), F32),
        ),
        scratch_shapes=[
            pltpu.VMEM((FFN_HALO + tm, 2 * d_ff), F32),
            pltpu.VMEM((tm, d_ff), BF16),
        ],
        compiler_params=pltpu.CompilerParams(
            dimension_semantics=("arbitrary", "arbitrary"), vmem_limit_bytes=VMEM_LIMIT_BYTES),
        name="outproj_convffn",
    )(x, o, cc, p["w_out"], p["g_ffn"], p["w_up"], p["w_ffn_dw"], p["b_ffn_dw"], p["w_down"], ffn_state)


def _pad_rows_front(a, n):
    return jnp.pad(a, ((0, 0), (n - a.shape[1], 0), (0, 0)))


KEY_TILE = 256
INPROJ_TILE = 512
INPROJ_ROW_GROUP = 128
FFN_TILE = 512


def _tiling(t):
    if t % INPROJ_TILE == 0:
        return INPROJ_TILE, INPROJ_ROW_GROUP, KEY_TILE, FFN_TILE, KEY_TILE
    assert t % SUBLANES == 0 and CONV_WIDTH - 1 <= t <= LANES
    return t, t, t, t, LANES


def _layer(x, past_k, past_v, conv_state, ffn_state, p):
    b, t, _ = x.shape
    tm, sub, tv, ffn_tm, tq = _tiling(t)
    tk, heads = KEY_TILE, SB_HEADS
    q, kb, k, v, vt, cc, cso = _inproj(x, _pad_rows_front(conv_state, CONV_HALO), p, tm=tm, sub=sub, tv=tv)
    if past_k is None:
        past = 0
        stop = (BF16_ROUNDING_SLACK * p["q_norm_bound"] * p["k_norm_bound"] + F32_EXP_UNDERFLOW).reshape(1)
        o = _attention(q, kb, vt, stop, tq=tq, tk=tk, past=0, heads=heads)
    else:
        past = past_k.shape[1]
        t_pad = -(-t // tk) * tk
        k_all = jnp.pad(jnp.concatenate([past_k.astype(BF16), kb], axis=1), ((0, 0), (0, t_pad - t), (0, 0)))
        v_all = jnp.pad(jnp.concatenate([past_v.astype(BF16), v.astype(BF16)], axis=1), ((0, 0), (0, t_pad - t), (0, 0)))
        vt_all = v_all.reshape(b, (past + t_pad) // tk, tk, SB_DIM).swapaxes(2, 3)
        q_pad = jnp.pad(q, ((0, 0), (0, tq - t), (0, 0)))
        cached = past_k.reshape(b, past, SB_HEADS, SB_HEAD_DIM)
        k_norm = jnp.sqrt(jnp.max(jnp.sum(cached * cached, axis=-1)))
        stop = (BF16_ROUNDING_SLACK * p["q_norm_bound"] * jnp.maximum(k_norm, p["k_norm_bound"])
                + F32_EXP_UNDERFLOW).reshape(1)
        o = _attention(q_pad, k_all, vt_all, stop, tq=tq, tk=tk, past=past, heads=heads)[:, :t]
    y, fso = _ffn(x, o, cc, _pad_rows_front(ffn_state, FFN_HALO), p, tm=ffn_tm)
    return (y, k.reshape(b, t, SB_HEADS, SB_HEAD_DIM), v.reshape(b, t, SB_HEADS, SB_HEAD_DIM),
            cso[:, CONV_HALO - (CONV_WIDTH - 1):], fso[:, FFN_HALO - (FFN_CONV_WIDTH - 1):])


def kernel(x_prompt, x_sample, cache_sb_k, cache_sb_v, state_conv, state_ffn_conv, g_mix, w_in, g_q, g_k, w_dw, b_dw, g_conv_ln, b_conv_ln, w_out, g_ffn, w_up, w_ffn_dw, b_ffn_dw, w_down):
    depth = g_mix.shape[0]
    assert depth == 1
    bp = x_prompt.shape[0]
    bs, plen = cache_sb_k.shape[1:3]
    d_ff = w_down.shape[1]
    head_of = jnp.arange(SB_DIM) // SB_HEAD_DIM
    row = lambda a: a.reshape(1, -1)
    p = {
        "g_mix": row(g_mix[0]), "w_in": _to_bf16(w_in[0]),
        "g_q": row(jnp.tile(g_q[0], SB_HEADS)) * SB_HEAD_DIM ** -0.5, "g_k": row(jnp.tile(g_k[0], SB_HEADS)),
        "grp": ((head_of[:, None] == head_of[None, :]).astype(F32) / SB_HEAD_DIM).astype(BF16),
        "w_dw": w_dw[0], "b_dw": row(b_dw[0]),
        "g_conv_ln": row(g_conv_ln[0]), "b_conv_ln": row(b_conv_ln[0]),
        "w_out": _to_bf16(w_out[0]), "g_ffn": row(g_ffn[0]),
        "w_up": _to_bf16(w_up[0]), "w_ffn_dw": w_ffn_dw[0], "b_ffn_dw": row(b_ffn_dw[0]),
        "w_down": _to_bf16(w_down[0]),
        "q_norm_bound": jnp.max(jnp.abs(g_q[0])),
        "k_norm_bound": SB_HEAD_DIM ** 0.5 * jnp.max(jnp.abs(g_k[0])),
    }
    zc = jnp.zeros((bp, CONV_WIDTH - 1, CONV_DIM), F32)
    zf = jnp.zeros((bp, FFN_CONV_WIDTH - 1, 2 * d_ff), F32)
    yp, kp, vp, cp, fp = _layer(x_prompt, None, None, zc, zf, p)
    ys, ks, vs, cs, fs = _layer(
        x_sample, cache_sb_k[0].reshape(bs, plen, SB_DIM), cache_sb_v[0].reshape(bs, plen, SB_DIM),
        state_conv[0], state_ffn_conv[0], p)
    return (yp, ys, kp[None], vp[None], ks[None], vs[None], cp[None], cs[None], fp[None], fs[None])
```

```python
import functools

import jax
import jax.numpy as jnp
from jax import lax
from jax.experimental import pallas as pl
from jax.experimental.pallas import tpu as pltpu

D_MODEL = 1024
SB_HEADS = 8
SB_HEAD_DIM = 64
SB_DIM = SB_HEADS * SB_HEAD_DIM
CONV_DIM = D_MODEL - SB_DIM
CONV_WIDTH = 31
FFN_CONV_WIDTH = 3
IN_DIM = 3 * SB_DIM + 2 * CONV_DIM
EPS = 1e-6
LOG2_E = 1.4426950408889634
F32_EXP_UNDERFLOW = 110.0
BF16_ROUNDING_SLACK = 1.02

LANES = 128
SUBLANES = 8
MXU_WIDTH = 256
HEADS_PER_VREG = LANES // SB_HEAD_DIM
CONV_HALO = 32
FFN_HALO = 8
VMEM_LIMIT_BYTES = 56 * 1024 * 1024
CAST_BLOCK_ROWS = 256

F32 = jnp.float32
BF16 = jnp.bfloat16


def _sigmoid(x):
    return 0.5 * jnp.tanh(0.5 * x) + 0.5


def _swish(x):
    half = 0.5 * x
    return half * jnp.tanh(half) + half


def _const_spec(shape):
    return pl.BlockSpec(shape, lambda *_: (0,) * len(shape), pipeline_mode=pl.Buffered(1))


def _to_bf16_kernel(w_ref, o_ref):
    o_ref[...] = w_ref[...].astype(BF16)


def _to_bf16(w):
    rows, cols = w.shape
    block = min(rows, CAST_BLOCK_ROWS)
    assert rows % block == 0
    return pl.pallas_call(
        _to_bf16_kernel,
        grid=(rows // block,),
        in_specs=[pl.BlockSpec((block, cols), lambda i: (i, 0))],
        out_specs=pl.BlockSpec((block, cols), lambda i: (i, 0)),
        out_shape=jax.ShapeDtypeStruct((rows, cols), BF16),
        compiler_params=pltpu.CompilerParams(
            dimension_semantics=("arbitrary",), vmem_limit_bytes=VMEM_LIMIT_BYTES),
        name="weight_to_bf16",
    )(w)


def _inproj_kernel(x_ref, gmix_ref, win_ref, gq_ref, gk_ref, grp_ref, wdw_ref, bdw_ref, gln_ref,
                   bln_ref, cst_ref, q_ref, kb_ref, k_ref, v_ref, vt_ref, cc_ref, cso_ref,
                   cbuf, conv_buf, *, tm, sub, tv, rows):
    @pl.when(pl.program_id(1) == 0)
    def _():
        cbuf[0:CONV_HALO, :] = cst_ref[0]

    def project(m0):
        x = x_ref[0, m0:m0 + sub, :]
        ms = jnp.mean(x * x, axis=-1, keepdims=True)
        xn = (x * lax.rsqrt(ms + EPS) * gmix_ref[...]).astype(BF16)
        return jnp.dot(xn, win_ref[...], preferred_element_type=F32)

    def head_rmsnorm(a, g):
        msq = jnp.dot((a * a).astype(BF16), grp_ref[...], preferred_element_type=F32)
        return a * lax.rsqrt(msq + EPS) * g

    def finish(m0, proj):
        out_rows = slice(m0, m0 + sub)
        qn = head_rmsnorm(proj[:, 0:SB_DIM], gq_ref[...])
        kn = head_rmsnorm(proj[:, SB_DIM:2 * SB_DIM], gk_ref[...])
        v = proj[:, 2 * SB_DIM:3 * SB_DIM]
        q_ref[0, out_rows, :] = qn.astype(BF16)
        kb_ref[0, out_rows, :] = kn.astype(BF16)
        k_ref[0, out_rows, :] = kn
        v_ref[0, out_rows, :] = v
        vt_ref[0, m0 // tv, :, m0 % tv:m0 % tv + sub] = v.T.astype(BF16)

        u = proj[:, 3 * SB_DIM:3 * SB_DIM + CONV_DIM]
        gate = proj[:, 3 * SB_DIM + CONV_DIM:]
        cbuf[CONV_HALO + m0:CONV_HALO + m0 + sub, :] = u * _sigmoid(gate)

        first = CONV_HALO - (CONV_WIDTH - 1)
        n_win = rows + CONV_HALO
        for r0 in range(m0, m0 + sub, rows):
            for l0 in range(0, CONV_DIM, LANES):
                win = cbuf[r0:r0 + n_win, l0:l0 + LANES]
                acc = jnp.broadcast_to(bdw_ref[:, l0:l0 + LANES], (rows, LANES))
                for phase in range(SUBLANES):
                    shifted = win if phase == 0 else pltpu.roll(win, n_win - phase, axis=0)
                    for base in range(0, CONV_HALO + 1, SUBLANES):
                        j = base + phase - first
                        if 0 <= j < CONV_WIDTH:
                            acc = acc + wdw_ref[j:j + 1, l0:l0 + LANES] * shifted[base:base + rows, :]
                conv_buf[r0:r0 + rows, l0:l0 + LANES] = acc

        c = conv_buf[out_rows, :]
        mu = jnp.mean(c, axis=-1, keepdims=True)
        xc = c - mu
        var = jnp.mean(xc * xc, axis=-1, keepdims=True)
        y = xc * lax.rsqrt(var + EPS) * gln_ref[...] + bln_ref[...]
        cc_ref[0, out_rows, :] = _swish(y).astype(BF16)

    starts = list(range(0, tm, sub))
    proj = project(starts[0])
    for i, m0 in enumerate(starts):
        nxt = project(starts[i + 1]) if i + 1 < len(starts) else None
        finish(m0, proj)
        proj = nxt

    tail = cbuf[tm:tm + CONV_HALO, :]
    cso_ref[0] = tail
    cbuf[0:CONV_HALO, :] = tail


def _inproj(x, conv_state, p, *, tm, sub, tv):
    b, t, _ = x.shape
    nt = t // tm
    rows = min(sub, 64)
    assert tm % tv == 0 and tv % sub == 0
    row_spec = lambda w: pl.BlockSpec((1, tm, w), lambda i, j: (i, j, 0))
    out_shape = (
        jax.ShapeDtypeStruct((b, t, SB_DIM), BF16),
        jax.ShapeDtypeStruct((b, t, SB_DIM), BF16),
        jax.ShapeDtypeStruct((b, t, SB_DIM), F32),
        jax.ShapeDtypeStruct((b, t, SB_DIM), F32),
        jax.ShapeDtypeStruct((b, t // tv, SB_DIM, tv), BF16),
        jax.ShapeDtypeStruct((b, t, CONV_DIM), BF16),
        jax.ShapeDtypeStruct((b, CONV_HALO, CONV_DIM), F32),
    )
    return pl.pallas_call(
        functools.partial(_inproj_kernel, tm=tm, sub=sub, tv=tv, rows=rows),
        grid=(b, nt),
        in_specs=[
            row_spec(D_MODEL),
            _const_spec((1, D_MODEL)),
            _const_spec((D_MODEL, IN_DIM)),
            _const_spec((1, SB_DIM)),
            _const_spec((1, SB_DIM)),
            _const_spec((SB_DIM, SB_DIM)),
            _const_spec((CONV_WIDTH, CONV_DIM)),
            _const_spec((1, CONV_DIM)),
            _const_spec((1, CONV_DIM)),
            _const_spec((1, CONV_DIM)),
            pl.BlockSpec((1, CONV_HALO, CONV_DIM), lambda i, j: (i, 0, 0)),
        ],
        out_specs=(
            row_spec(SB_DIM), row_spec(SB_DIM), row_spec(SB_DIM), row_spec(SB_DIM),
            pl.BlockSpec((1, tm // tv, SB_DIM, tv), lambda i, j: (i, j, 0, 0)),
            row_spec(CONV_DIM),
            pl.BlockSpec((1, CONV_HALO, CONV_DIM), lambda i, j: (i, 0, 0)),
        ),
        out_shape=out_shape,
        scratch_shapes=[
            pltpu.VMEM((CONV_HALO + tm, CONV_DIM), F32),
            pltpu.VMEM((tm, CONV_DIM), F32),
        ],
        compiler_params=pltpu.CompilerParams(
            dimension_semantics=("arbitrary", "arbitrary"), vmem_limit_bytes=VMEM_LIMIT_BYTES),
        name="inproj_conv",
    )(x, p["g_mix"], p["w_in"], p["g_q"], p["g_k"], p["grp"], p["w_dw"], p["b_dw"],
      p["g_conv_ln"], p["b_conv_ln"], conv_state)


def _attn_kernel(stop_ref, q_ref, k_ref, vt_ref, o_ref, qh_ref, acc_ref, carry_ref, *, tq, tk, past, heads, qt):
    for s in range(qt):
        _attn_q_tile(stop_ref, q_ref, k_ref, vt_ref, o_ref, qh_ref, acc_ref, carry_ref, tq=tq, tk=tk,
                     past=past, heads=heads, tile=pl.program_id(2) * qt + s, q_rows=slice(s * tq, (s + 1) * tq))


def _attn_q_tile(stop_ref, q_ref, k_ref, vt_ref, o_ref, qh_ref, acc_ref, carry_ref, *, tq, tk, past, heads,
                 tile, q_rows):
    q0 = past + tile * tq
    n_full = q0 // tk

    for h in range(heads):
        l0 = (h // HEADS_PER_VREG) * LANES
        q2 = q_ref[0, q_rows, l0:l0 + LANES]
        lane_head = lax.broadcasted_iota(jnp.int32, q2.shape, 1) // SB_HEAD_DIM
        qh_ref[h] = jnp.where(lane_head == h % HEADS_PER_VREG, q2, jnp.zeros_like(q2))
    suffix = (lax.broadcasted_iota(jnp.int32, (tk, tk), 1)
              >= lax.broadcasted_iota(jnp.int32, (tk, tk), 0)).astype(BF16)

    acc_ref[...] = jnp.zeros_like(acc_ref)
    carry_ref[...] = jnp.zeros_like(carry_ref)

    def key_tile(j, masked):
        rows = pl.ds(pl.multiple_of(j * tk, tk), tk)
        if masked:
            kpos = j * tk + lax.broadcasted_iota(jnp.int32, (tk, tq), 0)
            qpos = q0 + lax.broadcasted_iota(jnp.int32, (tk, tq), 1)
            visible = kpos < qpos
        zs, sps, withins, laters = [], [], [], []
        for h in range(heads):
            l0 = (h // HEADS_PER_VREG) * LANES
            zs.append(lax.dot_general(k_ref[0, rows, l0:l0 + LANES], qh_ref[h],
                                      (((1,), (1,)), ((), ())), preferred_element_type=F32))
        for z in zs:
            sp = jnp.maximum(z, 0.0) + jnp.log(1.0 + jnp.exp2(jnp.abs(z) * (-LOG2_E)))
            sps.append(jnp.where(visible, sp, 0.0) if masked else sp)
        for sp in sps:
            withins.append(jnp.dot(suffix, sp.astype(BF16), preferred_element_type=F32))
        for h in range(heads):
            laters.append(carry_ref[h:h + 1, :])
        carries = [laters[h] + withins[h][0:1, :] for h in range(heads)]
        more = jnp.min(functools.reduce(jnp.minimum, carries)) < stop_ref[0]
        for h in range(heads):
            a = jnp.exp(zs[h] - (withins[h] + laters[h]))
            if masked:
                a = jnp.where(visible, a, 0.0)
            hs = slice(h * SB_HEAD_DIM, (h + 1) * SB_HEAD_DIM)
            acc_ref[hs, :] += jnp.dot(vt_ref[0, j, hs, :], a.astype(BF16), preferred_element_type=F32)
            carry_ref[h:h + 1, :] = carries[h]
        return more

    def cond(state):
        i, more = state
        return jnp.logical_and(i < n_full, more)

    def body(state):
        i, _ = state
        return i + 1, key_tile(n_full - 1 - i, False)

    lax.while_loop(cond, body, (jnp.int32(0), key_tile(n_full, True)))
    o_ref[0, q_rows, :] = acc_ref[...].T.astype(BF16)


def _attention(q, k_all, vt_all, stop, *, tq, tk, past, heads):
    b, t_q, _ = q.shape
    t_k = k_all.shape[1]
    nq = t_q // tq
    width = heads * SB_HEAD_DIM
    assert t_q % tq == 0 and past % tk == 0 and (tq == tk or nq == 1) and tq <= tk
    assert t_k % tk == 0 and t_k >= past + (nq - 1) * tq + tk
    assert heads % HEADS_PER_VREG == 0 and SB_HEADS % heads == 0
    qt = QUERY_TILES_PER_STEP if nq % QUERY_TILES_PER_STEP == 0 else 1
    return pl.pallas_call(
        functools.partial(_attn_kernel, tq=tq, tk=tk, past=past, heads=heads, qt=qt),
        grid=(b, SB_HEADS // heads, nq // qt),
        in_specs=[
            pl.BlockSpec(memory_space=pltpu.SMEM),
            pl.BlockSpec((1, qt * tq, width), lambda i, h, j: (i, j, h)),
            pl.BlockSpec((1, t_k, width), lambda i, h, j: (i, 0, h)),
            pl.BlockSpec((1, t_k // tk, width, tk), lambda i, h, j: (i, 0, h, 0)),
        ],
        out_specs=pl.BlockSpec((1, qt * tq, width), lambda i, h, j: (i, j, h)),
        out_shape=jax.ShapeDtypeStruct((b, t_q, SB_DIM), BF16),
        scratch_shapes=[
            pltpu.VMEM((heads, tq, LANES), BF16),
            pltpu.VMEM((width, tq), F32),
            pltpu.VMEM((heads, tq), F32),
        ],
        compiler_params=pltpu.CompilerParams(
            dimension_semantics=("arbitrary", "arbitrary", "arbitrary"), vmem_limit_bytes=VMEM_LIMIT_BYTES),
        name="sb_attention",
    )(stop, q, k_all, vt_all)


def _ffn_kernel(x_ref, o_ref, cc_ref, wout_ref, gffn_ref, wup_ref, wfdw_ref, bfdw_ref, wdown_ref,
                fst_ref, y_ref, fso_ref, up_buf, act_buf, *, tm, d_ff, fc):
    @pl.when(pl.program_id(1) == 0)
    def _():
        up_buf[0:FFN_HALO, :] = fst_ref[0]

    h = (x_ref[0]
         + jnp.dot(o_ref[0], wout_ref[0:SB_DIM, :], preferred_element_type=F32)
         + jnp.dot(cc_ref[0], wout_ref[SB_DIM:, :], preferred_element_type=F32))
    ms = jnp.mean(h * h, axis=-1, keepdims=True)
    hn = (h * lax.rsqrt(ms + EPS) * gffn_ref[...]).astype(BF16)
    up_buf[FFN_HALO:FFN_HALO + tm, :] = jnp.dot(hn, wup_ref[...], preferred_element_type=F32)

    first = FFN_HALO - (FFN_CONV_WIDTH - 1)

    def conv(c0):
        win = up_buf[:, c0:c0 + fc]
        out = jnp.broadcast_to(bfdw_ref[:, c0:c0 + fc], (tm, fc))
        for j in range(FFN_CONV_WIDTH):
            shift = FFN_CONV_WIDTH - 1 - j
            shifted = win if shift == 0 else pltpu.roll(win, shift, axis=0)
            out = out + wfdw_ref[j:j + 1, c0:c0 + fc] * shifted[FFN_HALO:FFN_HALO + tm, :]
        return out

    for c0 in range(0, d_ff, fc):
        a = conv(c0)
        g = conv(d_ff + c0)
        act_buf[:, c0:c0 + fc] = (_swish(g) * a).astype(BF16)
    y_ref[0] = h + jnp.dot(act_buf[...], wdown_ref[...], preferred_element_type=F32)

    tail = up_buf[tm:tm + FFN_HALO, :]
    fso_ref[0] = tail
    up_buf[0:FFN_HALO, :] = tail


def _ffn(x, o, cc, ffn_state, p, *, tm):
    b, t, _ = x.shape
    d_ff = p["w_down"].shape[0]
    fc = MXU_WIDTH
    assert d_ff % fc == 0 and t % tm == 0
    row_spec = lambda w: pl.BlockSpec((1, tm, w), lambda i, j: (i, j, 0))
    return pl.pallas_call(
        functools.partial(_ffn_kernel, tm=tm, d_ff=d_ff, fc=fc),
        grid=(b, t // tm),
        in_specs=[
            row_spec(D_MODEL), row_spec(SB_DIM), row_spec(CONV_DIM),
            _const_spec((D_MODEL, D_MODEL)),
            _const_spec((1, D_MODEL)),
            _const_spec((D_MODEL, 2 * d_ff)),
            _const_spec((FFN_CONV_WIDTH, 2 * d_ff)),
            _const_spec((1, 2 * d_ff)),
            _const_spec((d_ff, D_MODEL)),
            pl.BlockSpec((1, FFN_HALO, 2 * d_ff), lambda i, j: (i, 0, 0)),
        ],
        out_specs=(
            row_spec(D_MODEL),
            pl.BlockSpec((1, FFN_HALO, 2 * d_ff), lambda i, j: (i, 0, 0)),
        ),
        out_shape=(
            jax.ShapeDtypeStruct((b, t, D_MODEL), F32),
            jax.ShapeDtypeStruct((b, FFN_HALO, 2 * d_ff), F32),
        ),
        scratch_shapes=[
            pltpu.VMEM((FFN_HALO + tm, 2 * d_ff), F32),
            pltpu.VMEM((tm, d_ff), BF16),
        ],
        compiler_params=pltpu.CompilerParams(
            dimension_semantics=("arbitrary", "arbitrary"), vmem_limit_bytes=VMEM_LIMIT_BYTES),
        name="outproj_convffn",
    )(x, o, cc, p["w_out"], p["g_ffn"], p["w_up"], p["w_ffn_dw"], p["b_ffn_dw"], p["w_down"], ffn_state)


def _pad_rows_front(a, n):
    return jnp.pad(a, ((0, 0), (n - a.shape[1], 0), (0, 0)))


KEY_TILE = 256
QUERY_TILES_PER_STEP = 2
INPROJ_TILE = 512
INPROJ_ROW_GROUP = 128
FFN_TILE = 512


def _tiling(t):
    if t % INPROJ_TILE == 0:
        return INPROJ_TILE, INPROJ_ROW_GROUP, KEY_TILE, FFN_TILE, KEY_TILE
    assert t % SUBLANES == 0 and CONV_WIDTH - 1 <= t <= LANES
    return t, t, t, t, LANES


def _layer(x, past_k, past_v, conv_state, ffn_state, p):
    b, t, _ = x.shape
    tm, sub, tv, ffn_tm, tq = _tiling(t)
    tk, heads = KEY_TILE, SB_HEADS
    q, kb, k, v, vt, cc, cso = _inproj(x, _pad_rows_front(conv_state, CONV_HALO), p, tm=tm, sub=sub, tv=tv)
    if past_k is None:
        past = 0
        stop = (BF16_ROUNDING_SLACK * p["q_norm_bound"] * p["k_norm_bound"] + F32_EXP_UNDERFLOW).reshape(1)
        o = _attention(q, kb, vt, stop, tq=tq, tk=tk, past=0, heads=heads)
    else:
        past = past_k.shape[1]
        t_pad = -(-t // tk) * tk
        k_all = jnp.pad(jnp.concatenate([past_k.astype(BF16), kb], axis=1), ((0, 0), (0, t_pad - t), (0, 0)))
        v_all = jnp.pad(jnp.concatenate([past_v.astype(BF16), v.astype(BF16)], axis=1), ((0, 0), (0, t_pad - t), (0, 0)))
        vt_all = v_all.reshape(b, (past + t_pad) // tk, tk, SB_DIM).swapaxes(2, 3)
        q_pad = jnp.pad(q, ((0, 0), (0, tq - t), (0, 0)))
        cached = past_k.reshape(b, past, SB_HEADS, SB_HEAD_DIM)
        k_norm = jnp.sqrt(jnp.max(jnp.sum(cached * cached, axis=-1)))
        stop = (BF16_ROUNDING_SLACK * p["q_norm_bound"] * jnp.maximum(k_norm, p["k_norm_bound"])
                + F32_EXP_UNDERFLOW).reshape(1)
        o = _attention(q_pad, k_all, vt_all, stop, tq=tq, tk=tk, past=past, heads=heads)[:, :t]
    y, fso = _ffn(x, o, cc, _pad_rows_front(ffn_state, FFN_HALO), p, tm=ffn_tm)
    return (y, k.reshape(b, t, SB_HEADS, SB_HEAD_DIM), v.reshape(b, t, SB_HEADS, SB_HEAD_DIM),
            cso[:, CONV_HALO - (CONV_WIDTH - 1):], fso[:, FFN_HALO - (FFN_CONV_WIDTH - 1):])


def kernel(x_prompt, x_sample, cache_sb_k, cache_sb_v, state_conv, state_ffn_conv, g_mix, w_in, g_q, g_k, w_dw, b_dw, g_conv_ln, b_conv_ln, w_out, g_ffn, w_up, w_ffn_dw, b_ffn_dw, w_down):
    depth = g_mix.shape[0]
    assert depth == 1
    bp = x_prompt.shape[0]
    bs, plen = cache_sb_k.shape[1:3]
    d_ff = w_down.shape[1]
    head_of = jnp.arange(SB_DIM) // SB_HEAD_DIM
    row = lambda a: a.reshape(1, -1)
    p = {
        "g_mix": row(g_mix[0]), "w_in": _to_bf16(w_in[0]),
        "g_q": row(jnp.tile(g_q[0], SB_HEADS)) * SB_HEAD_DIM ** -0.5, "g_k": row(jnp.tile(g_k[0], SB_HEADS)),
        "grp": ((head_of[:, None] == head_of[None, :]).astype(F32) / SB_HEAD_DIM).astype(BF16),
        "w_dw": w_dw[0], "b_dw": row(b_dw[0]),
        "g_conv_ln": row(g_conv_ln[0]), "b_conv_ln": row(b_conv_ln[0]),
        "w_out": _to_bf16(w_out[0]), "g_ffn": row(g_ffn[0]),
        "w_up": _to_bf16(w_up[0]), "w_ffn_dw": w_ffn_dw[0], "b_ffn_dw": row(b_ffn_dw[0]),
        "w_down": _to_bf16(w_down[0]),
        "q_norm_bound": jnp.max(jnp.abs(g_q[0])),
        "k_norm_bound": SB_HEAD_DIM ** 0.5 * jnp.max(jnp.abs(g_k[0])),
    }
    zc = jnp.zeros((bp, CONV_WIDTH - 1, CONV_DIM), F32)
    zf = jnp.zeros((bp, FFN_CONV_WIDTH - 1, 2 * d_ff), F32)
    yp, kp, vp, cp, fp = _layer(x_prompt, None, None, zc, zf, p)
    ys, ks, vs, cs, fs = _layer(
        x_sample, cache_sb_k[0].reshape(bs, plen, SB_DIM), cache_sb_v[0].reshape(bs, plen, SB_DIM),
        state_conv[0], state_ffn_conv[0], p)
    return (yp, ys, kp[None], vp[None], ks[None], vs[None], cp[None], cs[None], fp[None], fs[None])
```
